```python
import jax, jax.numpy as jnp
from jax import lax
import numpy as np

D_MODEL = 1024
BATCH = 8
SEQ = 2048
DEPTH = 2
DEC_BATCH = 128
DEC_SEQ = 4
PAST_LEN = 2048
PAGE_SIZE = 128

N_HEADS = 8
HEAD_DIM = D_MODEL // N_HEADS
IDX_HEADS = 8
IDX_DIM = 64
IDX_TOPK = 256
IDX_QBLOCK = 64
ATTN_SCALE = HEAD_DIM ** -0.5
ATTN_SPLITS = (N_HEADS * HEAD_DIM,
               2 * N_HEADS * HEAD_DIM,
               3 * N_HEADS * HEAD_DIM,
               3 * N_HEADS * HEAD_DIM + IDX_HEADS * IDX_DIM,
               3 * N_HEADS * HEAD_DIM + IDX_HEADS * IDX_DIM + IDX_DIM)
ATTN_IN = 3 * N_HEADS * HEAD_DIM + IDX_HEADS * IDX_DIM + IDX_DIM + IDX_HEADS
GM_CHUNK = 128
GM_WIDTH = D_MODEL
GM_GROUPS = 8
GM_GROUP_DIM = GM_WIDTH // GM_GROUPS
D_FF = 2816
N_EXPERTS = 8
TOP_K_EXPERTS = 2
D_FF_EXPERT = 3584
N_ATTN_LAYERS = (DEPTH + 1) // 2
N_GMLP_LAYERS = DEPTH // 2
EPS = 1e-6
NEG = -1e30

kernel_name = 'dsa_gmlp_hybrid_step'


def rms_norm(x, g):
    xf = x.astype(jnp.float32)
    y = xf * lax.rsqrt(jnp.mean(xf * xf, axis=-1, keepdims=True) + EPS)
    return (y * g.astype(jnp.float32)).astype(x.dtype)


def layer_norm(x, g, b):
    xf = x.astype(jnp.float32)
    mu = jnp.mean(xf, axis=-1, keepdims=True)
    var = jnp.mean(jnp.square(xf - mu), axis=-1, keepdims=True)
    y = (xf - mu) * lax.rsqrt(var + EPS)
    return (y * g.astype(jnp.float32) + b.astype(jnp.float32)).astype(x.dtype)


def swiglu(h, w_gate, w_up, w_down):
    return (jax.nn.silu(h @ w_gate) * (h @ w_up)) @ w_down


def attn_project(h, w_in, g_q, g_k, g_kidx):
    B, T, _ = h.shape
    q, k, v, qi, ki, wi = jnp.split(h @ w_in, ATTN_SPLITS, axis=-1)
    q = rms_norm(q.reshape(B, T, N_HEADS, HEAD_DIM), g_q)
    k = rms_norm(k.reshape(B, T, N_HEADS, HEAD_DIM), g_k)
    v = v.reshape(B, T, N_HEADS, HEAD_DIM)
    qi = qi.reshape(B, T, IDX_HEADS, IDX_DIM)
    ki = rms_norm(ki, g_kidx)
    return q, k, v, qi, ki, wi


def index_scores(qi, ki, wi):
    logits = jnp.einsum('bthd,bsd->bths', qi, ki).astype(jnp.float32) * IDX_DIM ** -0.5
    w = wi.astype(jnp.float32) * IDX_HEADS ** -0.5
    return jnp.einsum('bth,bths->bts', w, jax.nn.relu(logits))


def dsa_prompt(q, k, v, qi, ki, wi, topk):
    B, S = q.shape[0], q.shape[1]
    n_blocks = S // IDX_QBLOCK
    spos = jnp.arange(S)
    take_rows = jax.vmap(lambda rows, idx: rows[idx])

    def block(bi):
        t0 = bi * IDX_QBLOCK
        qb = lax.dynamic_slice_in_dim(q, t0, IDX_QBLOCK, axis=1)
        qib = lax.dynamic_slice_in_dim(qi, t0, IDX_QBLOCK, axis=1)
        wib = lax.dynamic_slice_in_dim(wi, t0, IDX_QBLOCK, axis=1)
        tpos = t0 + jnp.arange(IDX_QBLOCK)
        score = index_scores(qib, ki, wib)
        score = jnp.where(spos[None, None, :] <= tpos[None, :, None], score, NEG)
        _, idx = lax.top_k(score, topk)
        k_sel = take_rows(k, idx)
        v_sel = take_rows(v, idx)
        s = jnp.einsum('bthd,btkhd->bthk', qb, k_sel).astype(jnp.float32) * ATTN_SCALE
        valid = (idx <= tpos[None, :, None])[:, :, None, :]
        p = jax.nn.softmax(jnp.where(valid, s, NEG), axis=-1)
        return jnp.einsum('bthk,btkhd->bthd', p, v_sel).astype(q.dtype)

    o = lax.map(block, jnp.arange(n_blocks))
    return jnp.moveaxis(o, 0, 1).reshape(B, S, N_HEADS, HEAD_DIM)


def dsa_sample(q, k, v, qi, ki, wi, cache_k, cache_v, cache_kidx, page_table, topk):
    Bd, T = q.shape[0], q.shape[1]
    n_pages = page_table.shape[1]
    past = n_pages * PAGE_SIZE
    ki_past = cache_kidx[page_table].reshape(Bd, past, IDX_DIM)
    ki_all = jnp.concatenate([ki_past, ki.astype(ki_past.dtype)], axis=1)
    tpos = past + jnp.arange(T)
    spos = jnp.arange(past + T)
    score = index_scores(qi, ki_all, wi)
    score = jnp.where(spos[None, None, :] <= tpos[None, :, None], score, NEG)
    _, idx = lax.top_k(score, topk)
    valid = (idx <= tpos[None, :, None])[:, :, None, :]
    is_new = (idx >= past)[:, :, None, :]
    page_idx = jnp.minimum(idx // PAGE_SIZE, n_pages - 1)
    phys = jax.vmap(lambda pt, pi: pt[pi])(page_table, page_idx)
    off = idx % PAGE_SIZE
    k_sel = cache_k[phys, off]
    v_sel = cache_v[phys, off]
    onehot = (idx[..., None] == tpos[None, None, None, :]).astype(jnp.float32)
    s_pool = jnp.einsum('bthd,btkhd->bthk', q, k_sel).astype(jnp.float32)
    s_new_all = jnp.einsum('bthd,bjhd->bthj', q, k).astype(jnp.float32)
    s_new = jnp.einsum('bthj,btkj->bthk', s_new_all, onehot)
    s = jnp.where(is_new, s_new, s_pool) * ATTN_SCALE
    p = jax.nn.softmax(jnp.where(valid, s, NEG), axis=-1)
    p_pool = jnp.where(is_new, 0.0, p)
    p_new = jnp.einsum('bthk,btkj->bthj', p, onehot)
    o = jnp.einsum('bthk,btkhd->bthd', p_pool, v_sel) + jnp.einsum('bthj,bjhd->bthd', p_new, v)
    return o.astype(q.dtype)


def gmlp_mixer(h, w_in, ln_g, ln_b, w_s, b_s, w_out):
    B, T, _ = h.shape
    u, v = jnp.split(jax.nn.gelu(h @ w_in, approximate=False), 2, axis=-1)
    v = layer_norm(v, ln_g, ln_b)
    tc = min(T, GM_CHUNK)
    vg = v.reshape(B, T // tc, tc, GM_GROUPS, GM_GROUP_DIM)
    ws = w_s[:, :tc, :tc] * jnp.tril(jnp.ones((tc, tc), w_s.dtype))
    mixed = jnp.einsum('gij,bcjgd->bcigd', ws, vg) + b_s[:, :tc].T[None, None, :, :, None]
    return (u * mixed.reshape(B, T, GM_WIDTH)) @ w_out, v


def moe_swiglu(h, w_router, w_gate, w_up, w_down):
    shape = h.shape
    t = h.reshape(-1, shape[-1])
    logits = (t @ w_router).astype(jnp.float32)
    top_logits, top_idx = lax.top_k(logits, TOP_K_EXPERTS)
    gates = jax.nn.softmax(top_logits, axis=-1)
    combine = jnp.sum(jax.nn.one_hot(top_idx, N_EXPERTS, dtype=jnp.float32) * gates[..., None], axis=1)
    out = jnp.zeros_like(t)
    for e in range(N_EXPERTS):
        out = out + combine[:, e:e + 1].astype(t.dtype) * swiglu(t, w_gate[e], w_up[e], w_down[e])
    return out.reshape(shape)


def setup_inputs(seed: int = 0) -> dict:
    key = jax.random.key(seed)
    ks = jax.random.split(key, 26)
    n_pages = PAST_LEN // PAGE_SIZE
    n_pool = (DEC_BATCH * n_pages * 5) // 4
    NA, NB = N_ATTN_LAYERS, N_GMLP_LAYERS

    def nrm(k, shape, scale=1.0):
        return jax.random.normal(k, shape, jnp.float32) * scale

    page_table = jax.random.permutation(ks[5], n_pool)[: DEC_BATCH * n_pages]
    page_table = page_table.reshape(DEC_BATCH, n_pages).astype(jnp.int32)
    return {
        'x_prompt': nrm(ks[0], (BATCH, SEQ, D_MODEL)),
        'x_sample': nrm(ks[1], (DEC_BATCH, DEC_SEQ, D_MODEL)),
        'cache_k': nrm(ks[2], (NA, n_pool, PAGE_SIZE, N_HEADS, HEAD_DIM)),
        'cache_v': nrm(ks[3], (NA, n_pool, PAGE_SIZE, N_HEADS, HEAD_DIM)),
        'cache_kidx': nrm(ks[4], (NA, n_pool, PAGE_SIZE, IDX_DIM)),
        'page_table': page_table,
        'norm_mix': 1.0 + nrm(ks[6], (DEPTH, D_MODEL), 0.02),
        'norm_ffn': 1.0 + nrm(ks[7], (DEPTH, D_MODEL), 0.02),
        'attn_w_in': nrm(ks[8], (NA, D_MODEL, ATTN_IN), D_MODEL ** -0.5),
        'attn_g_q': 1.0 + nrm(ks[9], (NA, HEAD_DIM), 0.02),
        'attn_g_k': 1.0 + nrm(ks[10], (NA, HEAD_DIM), 0.02),
        'attn_g_kidx': 1.0 + nrm(ks[11], (NA, IDX_DIM), 0.02),
        'attn_w_out': nrm(ks[12], (NA, N_HEADS * HEAD_DIM, D_MODEL), (N_HEADS * HEAD_DIM) ** -0.5),
        'gm_w_in': nrm(ks[13], (NB, D_MODEL, 2 * GM_WIDTH), D_MODEL ** -0.5),
        'gm_ln_g': 1.0 + nrm(ks[14], (NB, GM_WIDTH), 0.02),
        'gm_ln_b': nrm(ks[15], (NB, GM_WIDTH), 0.02),
        'gm_w_s': nrm(ks[16], (NB, GM_GROUPS, GM_CHUNK, GM_CHUNK), GM_CHUNK ** -0.5),
        'gm_b_s': 1.0 + nrm(ks[17], (NB, GM_GROUPS, GM_CHUNK), 0.1),
        'gm_w_out': nrm(ks[18], (NB, GM_WIDTH, D_MODEL), GM_WIDTH ** -0.5),
        'ffn_w_gate': nrm(ks[19], (NA, D_MODEL, D_FF), D_MODEL ** -0.5),
        'ffn_w_up': nrm(ks[20], (NA, D_MODEL, D_FF), D_MODEL ** -0.5),
        'ffn_w_down': nrm(ks[21], (NA, D_FF, D_MODEL), D_FF ** -0.5),
        'moe_w_router': nrm(ks[22], (NB, D_MODEL, N_EXPERTS), D_MODEL ** -0.5),
        'moe_w_gate': nrm(ks[23], (NB, N_EXPERTS, D_MODEL, D_FF_EXPERT), D_MODEL ** -0.5),
        'moe_w_up': nrm(ks[24], (NB, N_EXPERTS, D_MODEL, D_FF_EXPERT), D_MODEL ** -0.5),
        'moe_w_down': nrm(ks[25], (NB, N_EXPERTS, D_FF_EXPERT, D_MODEL), D_FF_EXPERT ** -0.5),
    }


def reference(x_prompt, x_sample, cache_k, cache_v, cache_kidx, page_table,
              norm_mix, norm_ffn, attn_w_in, attn_g_q, attn_g_k, attn_g_kidx, attn_w_out,
              gm_w_in, gm_ln_g, gm_ln_b, gm_w_s, gm_b_s, gm_w_out,
              ffn_w_gate, ffn_w_up, ffn_w_down,
              moe_w_router, moe_w_gate, moe_w_up, moe_w_down):
    xp, xs = x_prompt, x_sample
    B, S, _ = xp.shape
    T = xs.shape[1]
    past = page_table.shape[1] * PAGE_SIZE
    topk_prompt = min(IDX_TOPK, S // 4)
    topk_sample = min(IDX_TOPK, (past + T) // 4)
    k_prompt, v_prompt, kidx_prompt = [], [], []
    k_sample, v_sample, kidx_sample, gm_v_sample = [], [], [], []
    for i in range(DEPTH):
        j = i // 2
        hp = rms_norm(xp, norm_mix[i])
        hs = rms_norm(xs, norm_mix[i])
        if i % 2 == 0:
            qp, kp, vp, qip, kip, wip = attn_project(hp, attn_w_in[j], attn_g_q[j], attn_g_k[j], attn_g_kidx[j])
            op = dsa_prompt(qp, kp, vp, qip, kip, wip, topk_prompt)
            qs, ks_, vs_, qis, kis, wis = attn_project(hs, attn_w_in[j], attn_g_q[j], attn_g_k[j], attn_g_kidx[j])
            os_ = dsa_sample(qs, ks_, vs_, qis, kis, wis, cache_k[j], cache_v[j], cache_kidx[j],
                             page_table, topk_sample)
            xp = xp + op.reshape(B, S, -1) @ attn_w_out[j]
            xs = xs + os_.reshape(xs.shape[0], T, -1) @ attn_w_out[j]
            n_pp = B * S // PAGE_SIZE
            k_prompt.append(kp.reshape(n_pp, PAGE_SIZE, N_HEADS, HEAD_DIM))
            v_prompt.append(vp.reshape(n_pp, PAGE_SIZE, N_HEADS, HEAD_DIM))
            kidx_prompt.append(kip.reshape(n_pp, PAGE_SIZE, IDX_DIM))
            k_sample.append(ks_)
            v_sample.append(vs_)
            kidx_sample.append(kis)
            xp = xp + swiglu(rms_norm(xp, norm_ffn[i]), ffn_w_gate[j], ffn_w_up[j], ffn_w_down[j])
            xs = xs + swiglu(rms_norm(xs, norm_ffn[i]), ffn_w_gate[j], ffn_w_up[j], ffn_w_down[j])
        else:
            op, _ = gmlp_mixer(hp, gm_w_in[j], gm_ln_g[j], gm_ln_b[j], gm_w_s[j], gm_b_s[j], gm_w_out[j])
            os_, v_gm = gmlp_mixer(hs, gm_w_in[j], gm_ln_g[j], gm_ln_b[j], gm_w_s[j], gm_b_s[j], gm_w_out[j])
            xp = xp + op
            xs = xs + os_
            gm_v_sample.append(v_gm)
            xp = xp + moe_swiglu(rms_norm(xp, norm_ffn[i]), moe_w_router[j], moe_w_gate[j], moe_w_up[j], moe_w_down[j])
            xs = xs + moe_swiglu(rms_norm(xs, norm_ffn[i]), moe_w_router[j], moe_w_gate[j], moe_w_up[j], moe_w_down[j])
    return (xp, xs, jnp.stack(k_prompt), jnp.stack(v_prompt), jnp.stack(kidx_prompt),
            jnp.stack(k_sample), jnp.stack(v_sample), jnp.stack(kidx_sample), jnp.stack(gm_v_sample))
```

```python
import functools
import math

import jax
import jax.numpy as jnp
from jax import lax
from jax.experimental import pallas as pl
from jax.experimental.pallas import tpu as pltpu

F32 = jnp.float32
BF16 = jnp.bfloat16
I32 = jnp.int32

N_HEADS = 8
HEAD_DIM = 128
IDX_HEADS = 8
IDX_DIM = 64
IDX_TOPK = 256
PAGE_SIZE = 128
GM_CHUNK = 128
GM_GROUPS = 8
N_EXPERTS = 8
EPS = 1e-6
NEG = -1e30

LANES = 128
MOE_ROWS = 128
VMEM_LIMIT = 56 * 1024 * 1024


def _cparams(*sem):
    return pltpu.CompilerParams(dimension_semantics=sem, vmem_limit_bytes=VMEM_LIMIT)


def _dot(a, b):
    return jnp.dot(a, b, preferred_element_type=F32)


def _dot_nt(a, b):
    return lax.dot_general(a, b, (((1,), (1,)), ((), ())), preferred_element_type=F32)


def _rms(x, g):
    return x * lax.rsqrt(jnp.mean(x * x, axis=-1, keepdims=True) + EPS) * g


def _float_key(x):
    x = jnp.where(x == 0.0, 0.0, x)
    b = lax.bitcast_convert_type(x, I32)
    return jnp.where(b >= 0, b, b ^ jnp.int32(0x7FFFFFFF))


def _count(m):
    return jnp.sum(jnp.where(m, 1.0, 0.0), axis=1, keepdims=True)


def _topk_select(key_ref, valid, k):
    R, C = key_ref.shape
    kf = jnp.float32(k)
    col = lax.broadcasted_iota(I32, (R, C), 1)
    int_min = jnp.int32(-2 ** 31)
    tau0 = jnp.where(_count(key_ref[...] >= 0) >= kf, jnp.int32(0), int_min)

    def bit_step(it, tau):
        cand = tau | (jnp.int32(1) << (30 - it))
        return jnp.where(_count(key_ref[...] >= cand) >= kf, cand, tau)

    tau = lax.fori_loop(0, 31, bit_step, tau0)
    key = key_ref[...]
    gt = key > tau
    eq = key == tau
    need = kf - _count(gt)
    excess = _count(eq & valid) - need
    n_steps = jnp.where(jnp.max(excess) > 0.0, C.bit_length(), 0)

    def col_step(_, lohi):
        lo, hi = lohi
        mid = (lo + hi) >> 1
        ok = _count((key_ref[...] == tau) & (col <= mid)) >= need
        return jnp.where(ok, lo, mid), jnp.where(ok, mid, hi)

    lo0 = jnp.full((R, 1), -1, I32)
    hi0 = jnp.full((R, 1), C - 1, I32)
    _, last = lax.fori_loop(0, n_steps, col_step, (lo0, hi0))
    return (gt | (eq & (col <= last))) & valid


def _attn_proj_kernel(x_ref, g_ref, w_ref, gq_ref, gk_ref, gki_ref,
                      q_ref, k_ref, v_ref, kb_ref, vb_ref, qi_ref, kw_ref):
    d = N_HEADS * HEAD_DIM
    h = _rms(x_ref[...], g_ref[...]).astype(BF16)
    scale = HEAD_DIM ** -0.5

    def head_norm(y, g):
        return y * lax.rsqrt(jnp.mean(y * y, axis=-1, keepdims=True) + EPS) * g

    yq = _dot(h, w_ref[:, 0:d])
    for hd in range(N_HEADS):
        sl = slice(hd * HEAD_DIM, (hd + 1) * HEAD_DIM)
        q_ref[:, sl] = (head_norm(yq[:, sl], gq_ref[...]) * scale).astype(BF16)
    yk = _dot(h, w_ref[:, d:2 * d])
    for hd in range(N_HEADS):
        sl = slice(hd * HEAD_DIM, (hd + 1) * HEAD_DIM)
        kn = head_norm(yk[:, sl], gk_ref[...])
        k_ref[:, sl] = kn
        kb_ref[:, sl] = kn.astype(BF16)
    yv = _dot(h, w_ref[:, 2 * d:3 * d])
    v_ref[...] = yv
    vb_ref[...] = yv.astype(BF16)
    ni = IDX_HEADS * IDX_DIM
    qi_ref[...] = (_dot(h, w_ref[:, 3 * d:3 * d + ni]) * (IDX_DIM ** -0.5)).astype(BF16)
    yt = _dot(h, w_ref[:, 3 * d + ni:3 * d + ni + LANES])
    lane = lax.broadcasted_iota(I32, yt.shape, 1)
    is_ki = lane < IDX_DIM
    ms = jnp.sum(jnp.where(is_ki, yt * yt, 0.0), axis=-1, keepdims=True) * (1.0 / IDX_DIM)
    kw_ref[...] = jnp.where(is_ki, yt * lax.rsqrt(ms + EPS) * gki_ref[...], yt * (IDX_HEADS ** -0.5))


def _attn_project(x, g_mix, w_in, g_q, g_k, g_kidx, tm):
    T, D = x.shape
    d = N_HEADS * HEAD_DIM
    n_in = w_in.shape[1]
    n_pad = 3 * d + IDX_HEADS * IDX_DIM + LANES
    w = jnp.pad(w_in, ((0, 0), (0, n_pad - n_in))).astype(BF16)
    gki = jnp.pad(g_kidx, (0, LANES - IDX_DIM)).reshape(1, LANES)
    row = lambda i: (i, 0)
    const = lambda i: (0, 0)
    return pl.pallas_call(
        _attn_proj_kernel,
        grid=(T // tm,),
        in_specs=[pl.BlockSpec((tm, D), row), pl.BlockSpec((1, D), const),
                  pl.BlockSpec((D, n_pad), const), pl.BlockSpec((1, HEAD_DIM), const),
                  pl.BlockSpec((1, HEAD_DIM), const), pl.BlockSpec((1, LANES), const)],
        out_specs=[pl.BlockSpec((tm, d), row), pl.BlockSpec((tm, d), row), pl.BlockSpec((tm, d), row),
                   pl.BlockSpec((tm, d), row), pl.BlockSpec((tm, d), row),
                   pl.BlockSpec((tm, IDX_HEADS * IDX_DIM), row), pl.BlockSpec((tm, LANES), row)],
        out_shape=[jax.ShapeDtypeStruct((T, d), BF16), jax.ShapeDtypeStruct((T, d), F32),
                   jax.ShapeDtypeStruct((T, d), F32), jax.ShapeDtypeStruct((T, d), BF16),
                   jax.ShapeDtypeStruct((T, d), BF16),
                   jax.ShapeDtypeStruct((T, IDX_HEADS * IDX_DIM), BF16),
                   jax.ShapeDtypeStruct((T, LANES), F32)],
        compiler_params=_cparams("arbitrary"),
        name="attn_project",
    )(x, g_mix.reshape(1, D), w, g_q.reshape(1, HEAD_DIM), g_k.reshape(1, HEAD_DIM), gki)


def _index_scores(qi, ki, w_cols):
    score = None
    for h in range(IDX_HEADS):
        logit = _dot_nt(qi[:, h * IDX_DIM:(h + 1) * IDX_DIM], ki)
        term = jnp.maximum(logit, 0.0) * w_cols(h)
        score = term if score is None else score + term
    return score


def _prompt_attn_kernel(q_ref, qi_ref, kwq_ref, kwk_ref, kb_ref, vb_ref, o_ref, key_ref, sel_ref, *, topk):
    tq, S = key_ref.shape
    t0 = pl.program_id(1) * tq
    ki = kwk_ref[:, 0:IDX_DIM].astype(BF16)
    score = _index_scores(qi_ref[...], ki, lambda h: kwq_ref[:, IDX_DIM + h:IDX_DIM + h + 1])
    row = lax.broadcasted_iota(I32, (tq, S), 0) + t0
    col = lax.broadcasted_iota(I32, (tq, S), 1)
    valid = col <= row
    key_ref[...] = _float_key(jnp.where(valid, score, NEG))
    sel_ref[...] = jnp.where(_topk_select(key_ref, valid, topk), 1.0, 0.0)
    for h in range(N_HEADS):
        sl = slice(h * HEAD_DIM, (h + 1) * HEAD_DIM)
        s = _dot_nt(q_ref[:, sl], kb_ref[:, sl])
        s = jnp.where(sel_ref[...] != 0.0, s, NEG)
        m = jnp.max(s, axis=-1, keepdims=True)
        p = jnp.exp(s - m)
        l = jnp.sum(p, axis=-1, keepdims=True)
        o = _dot(p.astype(BF16), vb_ref[:, sl]) / l
        o_ref[:, sl] = o.astype(o_ref.dtype)


def _prompt_attention(q, qi, kw, kb, vb, B, S, tq, topk):
    d = N_HEADS * HEAD_DIM
    nq = S // tq
    qrow = lambda b, i: (b * nq + i, 0)
    brow = lambda b, i: (b, 0)
    return pl.pallas_call(
        functools.partial(_prompt_attn_kernel, topk=topk),
        grid=(B, nq),
        in_specs=[pl.BlockSpec((tq, d), qrow), pl.BlockSpec((tq, IDX_HEADS * IDX_DIM), qrow),
                  pl.BlockSpec((tq, LANES), qrow), pl.BlockSpec((S, LANES), brow),
                  pl.BlockSpec((S, d), brow), pl.BlockSpec((S, d), brow)],
        out_specs=pl.BlockSpec((tq, d), qrow),
        out_shape=jax.ShapeDtypeStruct((B * S, d), BF16),
        scratch_shapes=[pltpu.VMEM((tq, S), I32), pltpu.VMEM((tq, S), F32)],
        compiler_params=_cparams("arbitrary", "arbitrary"),
        name="prompt_attention",
    )(q, qi, kw, kw, kb, vb)


def _sample_scores_kernel(pt_ref, qi_ref, w_ref, kin_ref, *rest, n_pages, n_new):
    page_refs = rest[:n_pages]
    s_ref = rest[n_pages]
    kall_ref = rest[n_pages + 1]
    past = n_pages * PAGE_SIZE
    C = kall_ref.shape[0]
    for p in range(n_pages):
        kall_ref[p * PAGE_SIZE:(p + 1) * PAGE_SIZE, :] = page_refs[p][...].astype(BF16)
    kall_ref[past:C, :] = kin_ref[...].astype(BF16)
    logit = _dot_nt(qi_ref[...], kall_ref[...])
    term = jnp.maximum(logit, 0.0) * w_ref[...]
    col = lax.broadcasted_iota(I32, (1, C), 1)
    for t in range(n_new):
        s_t = jnp.sum(term[t * IDX_HEADS:(t + 1) * IDX_HEADS, :], axis=0, keepdims=True)
        s_ref[t:t + 1, :] = jnp.where(col <= past + t, s_t, NEG)


def _sample_scores(page_table, qi_s, wi_s, ki_new, cache_kidx_l, C):
    Bd, n_pages = page_table.shape
    n_new = qi_s.shape[1] // IDX_HEADS
    page_spec = lambda p: pl.BlockSpec((None, PAGE_SIZE, IDX_DIM), lambda b, pt: (pt[b, p], 0, 0))
    per_seq = lambda b, pt: (b, 0, 0)
    grid_spec = pltpu.PrefetchScalarGridSpec(
        num_scalar_prefetch=1,
        grid=(Bd,),
        in_specs=[pl.BlockSpec((None, n_new * IDX_HEADS, IDX_DIM), per_seq),
                  pl.BlockSpec((None, n_new * IDX_HEADS, 1), per_seq),
                  pl.BlockSpec((None, C - n_pages * PAGE_SIZE, IDX_DIM), per_seq)]
                 + [page_spec(p) for p in range(n_pages)],
        out_specs=pl.BlockSpec((None, n_new, C), per_seq),
        scratch_shapes=[pltpu.VMEM((C, IDX_DIM), BF16)],
    )
    return pl.pallas_call(
        functools.partial(_sample_scores_kernel, n_pages=n_pages, n_new=n_new),
        grid_spec=grid_spec,
        out_shape=jax.ShapeDtypeStruct((Bd, n_new, C), F32),
        compiler_params=_cparams("arbitrary"),
        name="sample_scores",
    )(page_table, qi_s, wi_s, ki_new, *([cache_kidx_l] * n_pages))


def _sample_topk_kernel(s_ref, sel_ref, key_ref, *, topk, past, n_new):
    R, C = key_ref.shape
    row = lax.broadcasted_iota(I32, (R, C), 0)
    col = lax.broadcasted_iota(I32, (R, C), 1)
    valid = col <= past + (row % n_new)
    key_ref[...] = _float_key(s_ref[...])
    sel_ref[...] = jnp.where(_topk_select(key_ref, valid, topk), 1.0, 0.0).astype(sel_ref.dtype)


def _sample_topk(scores, topk, past, n_new, tr):
    R, C = scores.shape
    blk = pl.BlockSpec((tr, C), lambda i: (i, 0))
    return pl.pallas_call(
        functools.partial(_sample_topk_kernel, topk=topk, past=past, n_new=n_new),
        grid=(R // tr,),
        in_specs=[blk],
        out_specs=blk,
        out_shape=jax.ShapeDtypeStruct((R, C), BF16),
        scratch_shapes=[pltpu.VMEM((tr, C), I32)],
        compiler_params=_cparams("arbitrary"),
        name="sample_topk",
    )(scores)


def _sample_attn_kernel(pt_ref, q_ref, sel_ref, seln_ref, kn_ref, vn_ref, *rest, pg, n_new):
    k_refs = rest[:pg]
    v_refs = rest[pg:2 * pg]
    o_ref, m_ref, l_ref, acc_ref = rest[2 * pg:]
    g = pl.program_id(1)
    rows = n_new * N_HEADS
    flat = PAGE_SIZE * N_HEADS

    @pl.when(g == 0)
    def _():
        m_ref[...] = jnp.full(m_ref.shape, NEG, F32)
        l_ref[...] = jnp.zeros(l_ref.shape, F32)
        acc_ref[...] = jnp.zeros(acc_ref.shape, F32)

    q = q_ref[...]

    def expand_rows(x):
        return jnp.concatenate(
            [jnp.broadcast_to(x[t:t + 1, :], (N_HEADS, x.shape[1])) for t in range(n_new)], axis=0)

    def update(s, keep, v):
        s = jnp.where(keep, s, NEG)
        m_old = m_ref[...]
        m_new = jnp.maximum(m_old, jnp.max(s, axis=-1, keepdims=True))
        alpha = jnp.exp(m_old - m_new)
        p = jnp.where(keep, jnp.exp(s - m_new), 0.0)
        l_ref[...] = alpha * l_ref[...] + jnp.sum(p, axis=-1, keepdims=True)
        acc_ref[...] = alpha * acc_ref[...] + _dot(p.astype(BF16), v)
        m_ref[...] = m_new

    e_r = lax.broadcasted_iota(I32, (PAGE_SIZE, flat), 0)
    e_c = lax.broadcasted_iota(I32, (PAGE_SIZE, flat), 1)
    spread = jnp.where(e_c // N_HEADS == e_r, 1.0, 0.0).astype(BF16)
    hrow = lax.broadcasted_iota(I32, (rows, flat), 0) % N_HEADS
    hcol = lax.broadcasted_iota(I32, (rows, flat), 1) % N_HEADS
    same_head = hrow == hcol
    for p in range(pg):
        kf = k_refs[p][...].astype(BF16)
        vf = v_refs[p][...].astype(BF16)
        s = _dot_nt(q, kf)
        sel_p = _dot(sel_ref[:, p * PAGE_SIZE:(p + 1) * PAGE_SIZE], spread)
        keep = same_head & (expand_rows(sel_p) > 0.5)
        update(s, keep, vf)

    @pl.when(g == pl.num_programs(1) - 1)
    def _():
        kn = kn_ref[...].astype(BF16)
        vn = vn_ref[...].astype(BF16)
        s = _dot_nt(q, kn)
        r2 = lax.broadcasted_iota(I32, (rows, rows), 0)
        c2 = lax.broadcasted_iota(I32, (rows, rows), 1)
        sn = seln_ref[...].astype(F32)
        seln = jnp.concatenate(
            [jnp.broadcast_to(sn[:, j:j + 1], (n_new, N_HEADS)) for j in range(n_new)], axis=1)
        keep = (r2 % N_HEADS == c2 % N_HEADS) & (expand_rows(seln) > 0.5)
        update(s, keep, vn)
        o_ref[...] = (acc_ref[...] / l_ref[...]).astype(o_ref.dtype)


def _sample_attention(page_table, q_s, sel, k_new, v_new, cache_k_l, cache_v_l, pg):
    Bd, n_pages = page_table.shape
    rows = q_s.shape[1]
    n_new = rows // N_HEADS
    C = sel.shape[2]
    past = n_pages * PAGE_SIZE
    flat = PAGE_SIZE * N_HEADS
    per_seq = lambda b, g, pt: (b, 0, 0)
    kv_spec = lambda p: pl.BlockSpec((None, flat, HEAD_DIM), lambda b, g, pt: (pt[b, g * pg + p], 0, 0))
    grid_spec = pltpu.PrefetchScalarGridSpec(
        num_scalar_prefetch=1,
        grid=(Bd, n_pages // pg),
        in_specs=[pl.BlockSpec((None, rows, HEAD_DIM), per_seq),
                  pl.BlockSpec((None, n_new, pg * PAGE_SIZE), lambda b, g, pt: (b, 0, g)),
                  pl.BlockSpec((None, n_new, LANES), lambda b, g, pt: (b, 0, past // LANES)),
                  pl.BlockSpec((None, rows, HEAD_DIM), per_seq),
                  pl.BlockSpec((None, rows, HEAD_DIM), per_seq)]
                 + [kv_spec(p) for p in range(pg)] + [kv_spec(p) for p in range(pg)],
        out_specs=pl.BlockSpec((None, rows, HEAD_DIM), per_seq),
        scratch_shapes=[pltpu.VMEM((rows, 1), F32), pltpu.VMEM((rows, 1), F32),
                        pltpu.VMEM((rows, HEAD_DIM), F32)],
    )
    return pl.pallas_call(
        functools.partial(_sample_attn_kernel, pg=pg, n_new=n_new),
        grid_spec=grid_spec,
        out_shape=jax.ShapeDtypeStruct((Bd, rows, HEAD_DIM), BF16),
        compiler_params=_cparams("arbitrary", "arbitrary"),
        name="sample_attention",
    )(page_table, q_s, sel, sel, k_new, v_new, *([cache_k_l] * pg), *([cache_v_l] * pg))


def _out_proj_kernel(o_ref, w_ref, x_ref, y_ref):
    y_ref[...] = x_ref[...] + _dot(o_ref[...], w_ref[...])


def _out_proj(o, w, x, tm):
    T, D = x.shape
    row = lambda i: (i, 0)
    return pl.pallas_call(
        _out_proj_kernel,
        grid=(T // tm,),
        in_specs=[pl.BlockSpec((tm, o.shape[1]), row), pl.BlockSpec(w.shape, lambda i: (0, 0)),
                  pl.BlockSpec((tm, D), row)],
        out_specs=pl.BlockSpec((tm, D), row),
        out_shape=jax.ShapeDtypeStruct((T, D), F32),
        compiler_params=_cparams("arbitrary"),
        name="attn_out_proj",
    )(o, w.astype(BF16), x)


def _ffn_kernel(x_ref, g_ref, wg_ref, wu_ref, wd_ref, y_ref, h_ref):
    f = pl.program_id(1)

    @pl.when(f == 0)
    def _():
        x = x_ref[...]
        h_ref[...] = _rms(x, g_ref[...]).astype(BF16)
        y_ref[...] = x

    h = h_ref[...]
    a = jax.nn.silu(_dot(h, wg_ref[...])) * _dot(h, wu_ref[...])
    y_ref[...] += _dot(a.astype(BF16), wd_ref[...])


def _ffn(x, g, w_gate, w_up, w_down, tm, tf):
    T, D = x.shape
    F = w_gate.shape[1]
    row = lambda i, f: (i, 0)
    return pl.pallas_call(
        _ffn_kernel,
        grid=(T // tm, F // tf),
        in_specs=[pl.BlockSpec((tm, D), row), pl.BlockSpec((1, D), lambda i, f: (0, 0)),
                  pl.BlockSpec((D, tf), lambda i, f: (0, f)), pl.BlockSpec((D, tf), lambda i, f: (0, f)),
                  pl.BlockSpec((tf, D), lambda i, f: (f, 0))],
        out_specs=pl.BlockSpec((tm, D), row),
        out_shape=jax.ShapeDtypeStruct((T, D), F32),
        scratch_shapes=[pltpu.VMEM((tm, D), BF16)],
        compiler_params=_cparams("arbitrary", "arbitrary"),
        name="dense_swiglu",
    )(x, g.reshape(1, D), w_gate.astype(BF16), w_up.astype(BF16), w_down.astype(BF16))


def _gmlp_kernel(x_ref, g_ref, win_ref, lng_ref, lnb_ref, ws_ref, bs_ref, wout_ref, y_ref, v_ref):
    tm, D = x_ref.shape
    W = lng_ref.shape[1]
    gd = W // GM_GROUPS
    x = x_ref[...]
    h = _rms(x, g_ref[...]).astype(BF16)
    z = _dot(h, win_ref[...])
    uv = 0.5 * z * (1.0 + lax.erf(z * math.sqrt(0.5)))
    u = uv[:, :W]
    v = uv[:, W:]
    mu = jnp.mean(v, axis=-1, keepdims=True)
    var = jnp.mean(jnp.square(v - mu), axis=-1, keepdims=True)
    v = (v - mu) * lax.rsqrt(var + EPS) * lng_ref[...] + lnb_ref[...]
    v_ref[...] = v
    vb = v.astype(BF16)
    ri = lax.broadcasted_iota(I32, (GM_CHUNK, GM_CHUNK), 0)
    ci = lax.broadcasted_iota(I32, (GM_CHUNK, GM_CHUNK), 1)
    causal = ci <= ri
    ws = [jnp.where(causal, ws_ref[g], 0.0).astype(BF16) for g in range(GM_GROUPS)]
    chunks = []
    for c in range(tm // GM_CHUNK):
        rs = slice(c * GM_CHUNK, (c + 1) * GM_CHUNK)
        mixed = jnp.concatenate(
            [_dot(ws[g], vb[rs, g * gd:(g + 1) * gd]) for g in range(GM_GROUPS)], axis=1)
        chunks.append(u[rs, :] * (mixed + bs_ref[...]))
    gated = jnp.concatenate(chunks, axis=0).astype(BF16)
    y_ref[...] = x + _dot(gated, wout_ref[...])


def _gmlp(x, g, w_in, ln_g, ln_b, w_s, b_s, w_out, tm, n_prompt_tiles, n_new):
    T, D = x.shape
    W = ln_g.shape[0]
    gd = W // GM_GROUPS
    eye = jnp.eye(GM_CHUNK // n_new, dtype=w_s.dtype)
    ws_sample = jax.vmap(lambda m: jnp.kron(eye, m))(w_s[:, :n_new, :n_new])
    ws_all = jnp.stack([w_s, ws_sample])
    bs_prompt = jnp.repeat(b_s.T, gd, axis=1)
    bs_sample = jnp.tile(bs_prompt[:n_new], (GM_CHUNK // n_new, 1))
    bs_all = jnp.stack([bs_prompt, bs_sample])
    kind = lambda i: jnp.where(i >= n_prompt_tiles, 1, 0)
    row = lambda i: (i, 0)
    const = lambda i: (0, 0)
    n_tiles = T // tm
    return pl.pallas_call(
        _gmlp_kernel,
        grid=(n_tiles,),
        in_specs=[pl.BlockSpec((tm, D), row), pl.BlockSpec((1, D), const),
                  pl.BlockSpec((D, 2 * W), const), pl.BlockSpec((1, W), const), pl.BlockSpec((1, W), const),
                  pl.BlockSpec((None, GM_GROUPS, GM_CHUNK, GM_CHUNK), lambda i: (kind(i), 0, 0, 0)),
                  pl.BlockSpec((None, GM_CHUNK, W), lambda i: (kind(i), 0, 0)),
                  pl.BlockSpec((W, D), const)],
        out_specs=[pl.BlockSpec((tm, D), row),
                   pl.BlockSpec((tm, W), lambda i: (jnp.maximum(i - n_prompt_tiles, 0), 0))],
        out_shape=[jax.ShapeDtypeStruct((T, D), F32),
                   jax.ShapeDtypeStruct(((n_tiles - n_prompt_tiles) * tm, W), F32)],
        compiler_params=_cparams("arbitrary"),
        name="gmlp_mixer",
    )(x, g.reshape(1, D), w_in.astype(BF16), ln_g.reshape(1, W), ln_b.reshape(1, W),
      ws_all, bs_all, w_out.astype(BF16))


def _router_kernel(x_ref, g_ref, wr_ref, h_ref, comb_ref, rank_ref, rankt_ref):
    tm = x_ref.shape[0]
    hf = _rms(x_ref[...], g_ref[...])
    h_ref[...] = hf.astype(BF16)
    logits = jnp.dot(hf, wr_ref[...], preferred_element_type=F32, precision=lax.Precision.HIGHEST)
    lane = lax.broadcasted_iota(I32, logits.shape, 1)
    logits = jnp.where(lane < N_EXPERTS, logits, -jnp.inf)
    m1 = jnp.max(logits, axis=-1, keepdims=True)
    i1 = jnp.min(jnp.where(logits == m1, lane, LANES), axis=-1, keepdims=True)
    rest = jnp.where(lane == i1, -jnp.inf, logits)
    m2 = jnp.max(rest, axis=-1, keepdims=True)
    i2 = jnp.min(jnp.where(rest == m2, lane, LANES), axis=-1, keepdims=True)
    e2 = jnp.exp(m2 - m1)
    den = 1.0 + e2
    comb = jnp.where(lane == i1, 1.0 / den, 0.0) + jnp.where(lane == i2, e2 / den, 0.0)
    comb_ref[...] = comb
    routed = jnp.where((lane == i1) | (lane == i2), 1.0, 0.0).astype(BF16)
    r = lax.broadcasted_iota(I32, (tm, tm), 0)
    c = lax.broadcasted_iota(I32, (tm, tm), 1)
    rank = _dot(jnp.where(c < r, 1.0, 0.0).astype(BF16), routed)
    rank = jnp.where(routed > 0, rank, -1.0)
    rank_ref[...] = rank
    rankt_ref[...] = rank.T


def _router(x, g, w_router, tm):
    T, D = x.shape
    wr = jnp.pad(w_router, ((0, 0), (0, LANES - N_EXPERTS)))
    row = lambda i: (i, 0)
    colb = lambda i: (0, i)
    const = lambda i: (0, 0)
    return pl.pallas_call(
        _router_kernel,
        grid=(T // tm,),
        in_specs=[pl.BlockSpec((tm, D), row), pl.BlockSpec((1, D), const), pl.BlockSpec((D, LANES), const)],
        out_specs=[pl.BlockSpec((tm, D), row), pl.BlockSpec((tm, LANES), row),
                   pl.BlockSpec((tm, LANES), row), pl.BlockSpec((LANES, tm), colb)],
        out_shape=[jax.ShapeDtypeStruct((T, D), BF16), jax.ShapeDtypeStruct((T, LANES), F32),
                   jax.ShapeDtypeStruct((T, LANES), F32), jax.ShapeDtypeStruct((LANES, T), F32)],
        compiler_params=_cparams("arbitrary"),
        name="moe_router",
    )(x, g.reshape(1, D), wr)


def _moe_kernel(cnt_ref, h_ref, comb_ref, rank_ref, rankt_ref, wg_ref, wu_ref, wd_ref, acc_ref, y_ref):
    e = pl.program_id(0)
    i = pl.program_id(1)
    tm = h_ref.shape[0]
    lane = lax.broadcasted_iota(I32, (tm, LANES), 1)
    pick = lane == e
    gate = jnp.sum(jnp.where(pick, comb_ref[...], 0.0), axis=-1, keepdims=True)
    rank_c = jnp.sum(jnp.where(pick, rank_ref[...], 0.0), axis=-1, keepdims=True)
    rank_r = rankt_ref[pl.ds(e, 1), :]
    y_ref[...] = acc_ref[...]
    n_blocks = (cnt_ref[e, i] + MOE_ROWS - 1) // MOE_ROWS

    def block(jb, carry):
        base = (jb * MOE_ROWS).astype(F32)
        slot_r = lax.broadcasted_iota(I32, (MOE_ROWS, tm), 0).astype(F32) + base
        take = jnp.where(rank_r == slot_r, 1.0, 0.0).astype(BF16)
        xs = _dot(take, h_ref[...]).astype(BF16)
        a = jax.nn.silu(_dot(xs, wg_ref[...])) * _dot(xs, wu_ref[...])
        y = _dot(a.astype(BF16), wd_ref[...])
        slot_c = lax.broadcasted_iota(I32, (tm, MOE_ROWS), 1).astype(F32) + base
        put = jnp.where(rank_c == slot_c, 1.0, 0.0).astype(BF16)
        y_ref[...] += gate * _dot(put, y.astype(BF16))
        return carry

    lax.fori_loop(0, n_blocks, block, 0)


def _moe(x, h, comb, rank, rankt, counts, w_gate, w_up, w_down, tm):
    T, D = x.shape
    E, _, F = w_gate.shape
    tok = lambda e, i, cnt: (i, 0)
    once = pl.Buffered(1)
    grid_spec = pltpu.PrefetchScalarGridSpec(
        num_scalar_prefetch=1,
        grid=(E, T // tm),
        in_specs=[pl.BlockSpec((tm, D), tok), pl.BlockSpec((tm, LANES), tok), pl.BlockSpec((tm, LANES), tok),
                  pl.BlockSpec((LANES, tm), lambda e, i, cnt: (0, i)),
                  pl.BlockSpec((None, D, F), lambda e, i, cnt: (e, 0, 0), pipeline_mode=once),
                  pl.BlockSpec((None, D, F), lambda e, i, cnt: (e, 0, 0), pipeline_mode=once),
                  pl.BlockSpec((None, F, D), lambda e, i, cnt: (e, 0, 0), pipeline_mode=once),
                  pl.BlockSpec((tm, D), tok)],
        out_specs=pl.BlockSpec((tm, D), tok),
    )
    return pl.pallas_call(
        _moe_kernel,
        grid_spec=grid_spec,
        out_shape=jax.ShapeDtypeStruct((T, D), F32),
        input_output_aliases={8: 0},
        compiler_params=_cparams("arbitrary", "arbitrary"),
        name="moe_experts",
    )(counts, h, comb, rank, rankt, w_gate.astype(BF16), w_up.astype(BF16), w_down.astype(BF16), x)


def _token_tile(n_prompt, n_sample):
    for tm in (512, 256, 128):
        if n_prompt % tm == 0 and n_sample % tm == 0:
            return tm
    raise ValueError("token counts must be multiples of 128")


def _largest_divisor(n, candidates):
    for c in candidates:
        if n % c == 0:
            return c
    return n


def kernel(x_prompt, x_sample, cache_k, cache_v, cache_kidx, page_table, norm_mix, norm_ffn, attn_w_in, attn_g_q, attn_g_k, attn_g_kidx, attn_w_out, gm_w_in, gm_ln_g, gm_ln_b, gm_w_s, gm_b_s, gm_w_out, ffn_w_gate, ffn_w_up, ffn_w_down, moe_w_router, moe_w_gate, moe_w_up, moe_w_down):
    B, S, D = x_prompt.shape
    Bd, n_new, _ = x_sample.shape
    n_pages = page_table.shape[1]
    past = n_pages * PAGE_SIZE
    assert norm_mix.shape[0] == 2 and D == N_HEADS * HEAD_DIM
    n_p, n_s = B * S, Bd * n_new
    T = n_p + n_s
    tm = _token_tile(n_p, n_s)
    topk_prompt = min(IDX_TOPK, S // 4)
    topk_sample = min(IDX_TOPK, (past + n_new) // 4)
    d = N_HEADS * HEAD_DIM

    x = jnp.concatenate([x_prompt.reshape(n_p, D), x_sample.reshape(n_s, D)], axis=0)

    q, k, v, kb, vb, qi, kw = _attn_project(x, norm_mix[0], attn_w_in[0], attn_g_q[0], attn_g_k[0],
                                            attn_g_kidx[0], tm)
    tq = _largest_divisor(S, (256, 128, 64))
    o_p = _prompt_attention(q, qi, kw, kb, vb, B, S, tq, topk_prompt)

    C = past + LANES
    qi_s = qi[n_p:].reshape(Bd, n_new * IDX_HEADS, IDX_DIM)
    wi_s = kw[n_p:, IDX_DIM:IDX_DIM + IDX_HEADS].reshape(Bd, n_new * IDX_HEADS, 1)
    ki_new = jnp.pad(kw[n_p:, :IDX_DIM].reshape(Bd, n_new, IDX_DIM), ((0, 0), (0, LANES - n_new), (0, 0)))
    scores = _sample_scores(page_table, qi_s, wi_s, ki_new, cache_kidx[0], C)
    sel = _sample_topk(scores.reshape(n_s, C), topk_sample, past, n_new,
                       _largest_divisor(n_s, (256, 128, 64, 32, 16)))
    flat = PAGE_SIZE * N_HEADS
    n_pool = cache_k.shape[1]
    o_s = _sample_attention(
        page_table, q[n_p:].reshape(Bd, n_new * N_HEADS, HEAD_DIM), sel.reshape(Bd, n_new, C),
        k[n_p:].reshape(Bd, n_new * N_HEADS, HEAD_DIM), v[n_p:].reshape(Bd, n_new * N_HEADS, HEAD_DIM),
        cache_k[0].reshape(n_pool, flat, HEAD_DIM), cache_v[0].reshape(n_pool, flat, HEAD_DIM),
        _largest_divisor(n_pages, (4, 2, 1)))
    o = jnp.concatenate([o_p, o_s.reshape(n_s, d)], axis=0)
    x = _out_proj(o, attn_w_out[0], x, tm)
    x = _ffn(x, norm_ffn[0], ffn_w_gate[0], ffn_w_up[0], ffn_w_down[0], tm,
             _largest_divisor(ffn_w_gate.shape[2], (1408, 1024, 512, 256, 128)))

    x, gm_v = _gmlp(x, norm_mix[1], gm_w_in[0], gm_ln_g[0], gm_ln_b[0], gm_w_s[0], gm_b_s[0], gm_w_out[0],
                    tm, n_p // tm, n_new)
    h, comb, rank, rankt = _router(x, norm_ffn[1], moe_w_router[0], tm)
    counts = jnp.sum((rankt[:N_EXPERTS] >= 0.0).reshape(N_EXPERTS, T // tm, tm), axis=-1).astype(I32)
    x = _moe(x, h, comb, rank, rankt, counts, moe_w_gate[0], moe_w_up[0], moe_w_down[0], tm)

    n_pp = n_p // PAGE_SIZE
    return (x[:n_p].reshape(B, S, D), x[n_p:].reshape(Bd, n_new, D),
            k[:n_p].reshape(1, n_pp, PAGE_SIZE, N_HEADS, HEAD_DIM),
            v[:n_p].reshape(1, n_pp, PAGE_SIZE, N_HEADS, HEAD_DIM),
            kw[:n_p, :IDX_DIM].reshape(1, n_pp, PAGE_SIZE, IDX_DIM),
            k[n_p:].reshape(1, Bd, n_new, N_HEADS, HEAD_DIM),
            v[n_p:].reshape(1, Bd, n_new, N_HEADS, HEAD_DIM),
            kw[n_p:, :IDX_DIM].reshape(1, Bd, n_new, IDX_DIM),
            gm_v.reshape(1, Bd, n_new, gm_ln_g.shape[1]))
```

```python
import functools
import math

import jax
import jax.numpy as jnp
from jax import lax
from jax.experimental import pallas as pl
from jax.experimental.pallas import tpu as pltpu

F32 = jnp.float32
BF16 = jnp.bfloat16
I32 = jnp.int32

N_HEADS = 8
HEAD_DIM = 128
IDX_HEADS = 8
IDX_DIM = 64
IDX_TOPK = 256
PAGE_SIZE = 128
GM_CHUNK = 128
GM_GROUPS = 8
N_EXPERTS = 8
EPS = 1e-6
NEG = -1e30

LANES = 128
MOE_ROWS = 144
MAX_PAGES_PER_STEP = 16
VMEM_LIMIT = 56 * 1024 * 1024


def _cparams(*sem):
    return pltpu.CompilerParams(dimension_semantics=sem, vmem_limit_bytes=VMEM_LIMIT)


def _dot(a, b):
    return jnp.dot(a, b, preferred_element_type=F32)


def _dot_nt(a, b):
    return lax.dot_general(a, b, (((1,), (1,)), ((), ())), preferred_element_type=F32)


def _rms(x, g):
    return x * lax.rsqrt(jnp.mean(x * x, axis=-1, keepdims=True) + EPS) * g


def _count(m):
    return jnp.sum(jnp.where(m, 1.0, 0.0), axis=1, keepdims=True)


def _topk_select(x_ref, valid, k):
    R, C = x_ref.shape
    kf = jnp.float32(k)
    col = lax.broadcasted_iota(I32, (R, C), 1)
    x = x_ref[...]
    lo = jnp.min(x, axis=1, keepdims=True)
    hi = jnp.max(x, axis=1, keepdims=True)
    cnt = jnp.full((R, 1), float(C), F32)

    def probe(state, pivot):
        lo, hi, cnt = state
        c = _count(x_ref[...] >= pivot)
        up = (c >= kf) & (pivot > lo)
        dn = (c < kf) & (pivot < hi)
        return jnp.where(up, pivot, lo), jnp.where(dn, pivot, hi), jnp.where(up, c, cnt)

    state = probe((lo, hi, cnt), hi)
    state = probe(state, jnp.full((R, 1), NEG * (1.0 - 2.0 ** -20), F32))
    state = probe(state, jnp.min(jnp.where(x > NEG, x, jnp.inf), axis=1, keepdims=True))

    def midpoint(lo, hi):
        return 0.5 * lo + 0.5 * hi

    def n_open(state):
        lo, hi, cnt = state
        mid = midpoint(lo, hi)
        return jnp.max(jnp.where((cnt > kf) & (mid > lo) & (mid < hi), 1.0, 0.0))

    def cond(carry):
        it, open_rows, _ = carry
        return (open_rows > 0.0) & (it < 512)

    def body(carry):
        it, _, state = carry
        state = probe(state, midpoint(state[0], state[1]))
        return it + 1, n_open(state), state

    _, _, (tau, _, _) = lax.while_loop(cond, body, (jnp.int32(0), n_open(state), state))
    gt = x > tau
    eq = x == tau
    need = kf - _count(gt)
    excess = _count(eq & valid) - need
    n_steps = jnp.where(jnp.max(excess) > 0.0, C.bit_length(), 0)

    def col_step(_, lohi):
        lo, hi = lohi
        mid = (lo + hi) >> 1
        ok = _count((x_ref[...] == tau) & (col <= mid)) >= need
        return jnp.where(ok, lo, mid), jnp.where(ok, mid, hi)

    lo0 = jnp.full((R, 1), -1, I32)
    hi0 = jnp.full((R, 1), C - 1, I32)
    _, last = lax.fori_loop(0, n_steps, col_step, (lo0, hi0))
    return (gt | (eq & (col <= last))) & valid


def _attn_proj_kernel(xp_ref, xs_ref, g_ref, w_ref, gq_ref, gk_ref, gki_ref,
                      q_ref, kb_ref, vb_ref, qi_ref, kw_ref, kp_ref, vp_ref, ks_ref, vs_ref, *, n_prompt_tiles):
    d = N_HEADS * HEAD_DIM
    tm = xp_ref.shape[0]
    is_prompt = pl.program_id(0) < n_prompt_tiles
    x = jnp.where(is_prompt, xp_ref[...], xs_ref[...])
    h = _rms(x, g_ref[...]).astype(BF16)
    scale = HEAD_DIM ** -0.5

    def head_norm(y, g):
        return y * lax.rsqrt(jnp.mean(y * y, axis=-1, keepdims=True) + EPS) * g

    def put_heads(y, p_ref, s_ref):
        @pl.when(is_prompt)
        def _():
            for hd in range(N_HEADS):
                p_ref[pl.ds(hd, tm, stride=N_HEADS), :] = y[hd]

        @pl.when(jnp.logical_not(is_prompt))
        def _():
            for hd in range(N_HEADS):
                s_ref[pl.ds(hd, tm, stride=N_HEADS), :] = y[hd]

    yq = _dot(h, w_ref[:, 0:d])
    for hd in range(N_HEADS):
        sl = slice(hd * HEAD_DIM, (hd + 1) * HEAD_DIM)
        q_ref[:, sl] = (head_norm(yq[:, sl], gq_ref[...]) * scale).astype(BF16)
    yk = _dot(h, w_ref[:, d:2 * d])
    kn = [head_norm(yk[:, hd * HEAD_DIM:(hd + 1) * HEAD_DIM], gk_ref[...]) for hd in range(N_HEADS)]
    put_heads(kn, kp_ref, ks_ref)
    yv = _dot(h, w_ref[:, 2 * d:3 * d])

    @pl.when(is_prompt)
    def _():
        for hd in range(N_HEADS):
            kb_ref[:, hd * HEAD_DIM:(hd + 1) * HEAD_DIM] = kn[hd].astype(BF16)
        vb_ref[...] = yv.astype(BF16)

    put_heads([yv[:, hd * HEAD_DIM:(hd + 1) * HEAD_DIM] for hd in range(N_HEADS)], vp_ref, vs_ref)
    ni = IDX_HEADS * IDX_DIM
    qi_ref[...] = (_dot(h, w_ref[:, 3 * d:3 * d + ni]) * (IDX_DIM ** -0.5)).astype(BF16)
    yt = _dot(h, w_ref[:, 3 * d + ni:3 * d + ni + LANES])
    lane = lax.broadcasted_iota(I32, yt.shape, 1)
    is_ki = lane < IDX_DIM
    ms = jnp.sum(jnp.where(is_ki, yt * yt, 0.0), axis=-1, keepdims=True) * (1.0 / IDX_DIM)
    kw_ref[...] = jnp.where(is_ki, yt * lax.rsqrt(ms + EPS) * gki_ref[...], yt * (IDX_HEADS ** -0.5))


def _attn_project(x_p, x_s, g_mix, w_in, g_q, g_k, g_kidx, tm):
    n_p, D = x_p.shape
    n_s = x_s.shape[0]
    T = n_p + n_s
    npt = n_p // tm
    d = N_HEADS * HEAD_DIM
    n_in = w_in.shape[1]
    n_pad = 3 * d + IDX_HEADS * IDX_DIM + LANES
    w = jnp.pad(w_in, ((0, 0), (0, n_pad - n_in))).astype(BF16)
    gki = jnp.pad(g_kidx, (0, LANES - IDX_DIM)).reshape(1, LANES)
    row = lambda i: (i, 0)
    prow = lambda i: (jnp.minimum(i, npt - 1), 0)
    srow = lambda i: (jnp.maximum(i - npt, 0), 0)
    const = lambda i: (0, 0)
    stream = lambda n, dt: jax.ShapeDtypeStruct((T, n), dt)
    return pl.pallas_call(
        functools.partial(_attn_proj_kernel, n_prompt_tiles=npt),
        grid=(T // tm,),
        in_specs=[pl.BlockSpec((tm, D), prow), pl.BlockSpec((tm, D), srow), pl.BlockSpec((1, D), const),
                  pl.BlockSpec((D, n_pad), const), pl.BlockSpec((1, HEAD_DIM), const),
                  pl.BlockSpec((1, HEAD_DIM), const), pl.BlockSpec((1, LANES), const)],
        out_specs=[pl.BlockSpec((tm, d), row), pl.BlockSpec((tm, d), prow), pl.BlockSpec((tm, d), prow),
                   pl.BlockSpec((tm, IDX_HEADS * IDX_DIM), row), pl.BlockSpec((tm, LANES), row),
                   pl.BlockSpec((tm * N_HEADS, HEAD_DIM), prow), pl.BlockSpec((tm * N_HEADS, HEAD_DIM), prow),
                   pl.BlockSpec((tm * N_HEADS, HEAD_DIM), srow), pl.BlockSpec((tm * N_HEADS, HEAD_DIM), srow)],
        out_shape=[stream(d, BF16), jax.ShapeDtypeStruct((n_p, d), BF16), jax.ShapeDtypeStruct((n_p, d), BF16),
                   stream(IDX_HEADS * IDX_DIM, BF16), stream(LANES, F32),
                   jax.ShapeDtypeStruct((n_p * N_HEADS, HEAD_DIM), F32),
                   jax.ShapeDtypeStruct((n_p * N_HEADS, HEAD_DIM), F32),
                   jax.ShapeDtypeStruct((n_s * N_HEADS, HEAD_DIM), F32),
                   jax.ShapeDtypeStruct((n_s * N_HEADS, HEAD_DIM), F32)],
        compiler_params=_cparams("arbitrary"),
        name="attn_project",
    )(x_p, x_s, g_mix.reshape(1, D), w, g_q.reshape(1, HEAD_DIM), g_k.reshape(1, HEAD_DIM), gki)


def _prompt_attn_kernel(*refs, topk, t_first, aliased):
    if aliased:
        refs = refs[1:]
    q_ref, qi_ref, kwq_ref, kwk_ref, kb_ref, vb_ref, o_ref, x_ref, sel_ref = refs
    tq, S = x_ref.shape
    t0 = t_first + pl.program_id(1) * tq
    ki = kwk_ref[:, 0:IDX_DIM].astype(BF16)
    qi = qi_ref[...]
    score = None
    for h in range(IDX_HEADS):
        logit = _dot_nt(qi[:, h * IDX_DIM:(h + 1) * IDX_DIM], ki)
        term = jnp.maximum(logit, 0.0) * kwq_ref[:, IDX_DIM + h:IDX_DIM + h + 1]
        score = term if score is None else score + term
    row = lax.broadcasted_iota(I32, (tq, S), 0) + t0
    col = lax.broadcasted_iota(I32, (tq, S), 1)
    valid = col <= row
    x_ref[...] = jnp.where(valid, score, NEG)
    sel_ref[...] = jnp.where(_topk_select(x_ref, valid, topk), 1.0, 0.0)
    for h in range(N_HEADS):
        sl = slice(h * HEAD_DIM, (h + 1) * HEAD_DIM)
        s = _dot_nt(q_ref[:, sl], kb_ref[:, sl])
        s = jnp.where(sel_ref[...] != 0.0, s, NEG)
        m = jnp.max(s, axis=-1, keepdims=True)
        p = jnp.exp(s - m)
        l = jnp.sum(p, axis=-1, keepdims=True)
        o = _dot(p.astype(BF16), vb_ref[:, sl]) / l
        o_ref[:, sl] = o.astype(o_ref.dtype)


def _prompt_attention(q, qi, kw, kb, vb, B, S, tq, topk):
    d = N_HEADS * HEAD_DIM
    nq = S // tq
    span = 2 if nq % 2 == 0 else 1
    kw3, kb3, vb3 = (a[:B * S].reshape(B, S, a.shape[1]) for a in (kw, kb, vb))
    o = None
    for j in range(nq // span):
        s_eff = (j + 1) * span * tq
        qrow = lambda b, i, j=j: (b * nq + j * span + i, 0)
        batch = lambda b, i: (b, 0, 0)
        in_specs = [pl.BlockSpec((tq, d), qrow), pl.BlockSpec((tq, IDX_HEADS * IDX_DIM), qrow),
                    pl.BlockSpec((tq, LANES), qrow), pl.BlockSpec((None, s_eff, LANES), batch),
                    pl.BlockSpec((None, s_eff, d), batch), pl.BlockSpec((None, s_eff, d), batch)]
        args = [q, qi, kw, kw3, kb3, vb3]
        aliases = {}
        if o is not None:
            in_specs = [pl.BlockSpec(memory_space=pl.ANY)] + in_specs
            args = [o] + args
            aliases = {0: 0}
        o = pl.pallas_call(
            functools.partial(_prompt_attn_kernel, topk=topk, t_first=j * span * tq, aliased=o is not None),
            grid=(B, span),
            in_specs=in_specs,
            out_specs=pl.BlockSpec((tq, d), qrow),
            out_shape=jax.ShapeDtypeStruct((B * S, d), BF16),
            scratch_shapes=[pltpu.VMEM((tq, s_eff), F32), pltpu.VMEM((tq, s_eff), F32)],
            input_output_aliases=aliases,
            compiler_params=_cparams("arbitrary", "arbitrary"),
            name=f"prompt_attention_{j}",
        )(*args)
    return o


def _sample_scores_kernel(pt_ref, qi_ref, w_ref, kin_ref, *rest, n_pages, n_new):
    page_refs = rest[:n_pages]
    s_ref = rest[n_pages]
    kall_ref = rest[n_pages + 1]
    past = n_pages * PAGE_SIZE
    C = kall_ref.shape[1]
    for p in range(n_pages):
        kall_ref[:, p * PAGE_SIZE:(p + 1) * PAGE_SIZE] = page_refs[p][...].astype(BF16)
    kall_ref[:, past:C] = kin_ref[...].astype(BF16)
    logit = _dot(qi_ref[...], kall_ref[...])
    term = jnp.maximum(logit, 0.0) * w_ref[...]
    col = lax.broadcasted_iota(I32, (1, C), 1)
    for t in range(n_new):
        s_t = jnp.sum(term[t * IDX_HEADS:(t + 1) * IDX_HEADS, :], axis=0, keepdims=True)
        s_ref[t:t + 1, :] = jnp.where(col <= past + t, s_t, NEG)


def _sample_scores(page_table, qi_s, wi_s, ki_new_t, kidx_t, C):
    Bd, n_pages = page_table.shape
    n_new = qi_s.shape[1] // IDX_HEADS
    page_spec = lambda p: pl.BlockSpec((None, IDX_DIM, PAGE_SIZE), lambda b, pt: (pt[b, p], 0, 0))
    per_seq = lambda b, pt: (b, 0, 0)
    grid_spec = pltpu.PrefetchScalarGridSpec(
        num_scalar_prefetch=1,
        grid=(Bd,),
        in_specs=[pl.BlockSpec((None, n_new * IDX_HEADS, IDX_DIM), per_seq),
                  pl.BlockSpec((None, n_new * IDX_HEADS, 1), per_seq),
                  pl.BlockSpec((None, IDX_DIM, C - n_pages * PAGE_SIZE), per_seq)]
                 + [page_spec(p) for p in range(n_pages)],
        out_specs=pl.BlockSpec((None, n_new, C), per_seq),
        scratch_shapes=[pltpu.VMEM((IDX_DIM, C), BF16)],
    )
    return pl.pallas_call(
        functools.partial(_sample_scores_kernel, n_pages=n_pages, n_new=n_new),
        grid_spec=grid_spec,
        out_shape=jax.ShapeDtypeStruct((Bd, n_new, C), F32),
        compiler_params=_cparams("arbitrary"),
        name="sample_scores",
    )(page_table, qi_s, wi_s, ki_new_t, *([kidx_t] * n_pages))


def _sample_topk_kernel(s_ref, selx_ref, seln_ref, x_ref, *, topk, past, n_new):
    R, C = x_ref.shape
    flat = PAGE_SIZE * N_HEADS
    row = lax.broadcasted_iota(I32, (R, C), 0)
    col = lax.broadcasted_iota(I32, (R, C), 1)
    valid = col <= past + (row % n_new)
    x_ref[...] = s_ref[...]
    sel = jnp.where(_topk_select(x_ref, valid, topk), 1.0, 0.0).astype(BF16)
    e_r = lax.broadcasted_iota(I32, (PAGE_SIZE, flat), 0)
    e_c = lax.broadcasted_iota(I32, (PAGE_SIZE, flat), 1)
    spread = jnp.where(e_c // N_HEADS == e_r, 1.0, 0.0).astype(BF16)
    for p in range(past // PAGE_SIZE):
        selx_ref[:, p * flat:(p + 1) * flat] = _dot(
            sel[:, p * PAGE_SIZE:(p + 1) * PAGE_SIZE], spread).astype(selx_ref.dtype)
    seln_ref[...] = sel[:, past:C]


def _sample_topk(scores, topk, past, n_new, tr):
    R, C = scores.shape
    wide = past * N_HEADS
    row = lambda i: (i, 0)
    return pl.pallas_call(
        functools.partial(_sample_topk_kernel, topk=topk, past=past, n_new=n_new),
        grid=(R // tr,),
        in_specs=[pl.BlockSpec((tr, C), row)],
        out_specs=[pl.BlockSpec((tr, wide), row), pl.BlockSpec((tr, C - past), row)],
        out_shape=[jax.ShapeDtypeStruct((R, wide), BF16), jax.ShapeDtypeStruct((R, C - past), BF16)],
        scratch_shapes=[pltpu.VMEM((tr, C), F32)],
        compiler_params=_cparams("arbitrary"),
        name="sample_topk",
    )(scores)


def _sample_attn_kernel(pt_ref, q_ref, selx_ref, seln_ref, kn_ref, vn_ref, *rest, pg, n_new):
    k_refs = rest[:pg]
    v_refs = rest[pg:2 * pg]
    o_ref, m_ref, l_ref, acc_ref = rest[2 * pg:]
    g = pl.program_id(1)
    rows = n_new * N_HEADS
    flat = PAGE_SIZE * N_HEADS

    @pl.when(g == 0)
    def _():
        m_ref[...] = jnp.full(m_ref.shape, NEG, F32)
        l_ref[...] = jnp.zeros(l_ref.shape, F32)
        acc_ref[...] = jnp.zeros(acc_ref.shape, F32)

    q = q_ref[...]

    def expand_rows(x):
        return jnp.concatenate(
            [jnp.broadcast_to(x[t:t + 1, :], (N_HEADS, x.shape[1])) for t in range(n_new)], axis=0)

    def update(scores, keeps, values):
        scores = [jnp.where(kp, s, NEG) for s, kp in zip(scores, keeps)]
        m_old = m_ref[...]
        m_new = m_old
        for s in scores:
            m_new = jnp.maximum(m_new, jnp.max(s, axis=-1, keepdims=True))
        alpha = jnp.exp(m_old - m_new)
        l_new = alpha * l_ref[...]
        acc = alpha * acc_ref[...]
        for s, kp, v in zip(scores, keeps, values):
            p = jnp.where(kp, jnp.exp(s - m_new), 0.0)
            l_new = l_new + jnp.sum(p, axis=-1, keepdims=True)
            acc = acc + _dot(p.astype(BF16), v)
        m_ref[...] = m_new
        l_ref[...] = l_new
        acc_ref[...] = acc

    hrow = lax.broadcasted_iota(I32, (rows, flat), 0) % N_HEADS
    hcol = lax.broadcasted_iota(I32, (rows, flat), 1) % N_HEADS
    same_head = hrow == hcol
    scores, keeps, values = [], [], []
    for p in range(pg):
        scores.append(_dot_nt(q, k_refs[p][...].astype(BF16)))
        sel_p = selx_ref[:, p * flat:(p + 1) * flat].astype(F32)
        keeps.append(same_head & (expand_rows(sel_p) > 0.5))
        values.append(v_refs[p][...].astype(BF16))
    update(scores, keeps, values)

    @pl.when(g == pl.num_programs(1) - 1)
    def _():
        s = _dot_nt(q, kn_ref[...].astype(BF16))
        r2 = lax.broadcasted_iota(I32, (rows, rows), 0)
        c2 = lax.broadcasted_iota(I32, (rows, rows), 1)
        sn = seln_ref[...].astype(F32)
        seln = jnp.concatenate(
            [jnp.broadcast_to(sn[:, j:j + 1], (n_new, N_HEADS)) for j in range(n_new)], axis=1)
        keep = (r2 % N_HEADS == c2 % N_HEADS) & (expand_rows(seln) > 0.5)
        update([s], [keep], [vn_ref[...].astype(BF16)])
        o_ref[...] = (acc_ref[...] / l_ref[...]).astype(o_ref.dtype)


def _sample_attention(page_table, q_s, selx, seln, k_new, v_new, cache_k_l, cache_v_l, pg):
    Bd, n_pages = page_table.shape
    rows = q_s.shape[1]
    n_new = rows // N_HEADS
    flat = PAGE_SIZE * N_HEADS
    per_seq = lambda b, g, pt: (b, 0, 0)
    kv_spec = lambda p: pl.BlockSpec((None, flat, HEAD_DIM), lambda b, g, pt: (pt[b, g * pg + p], 0, 0))
    grid_spec = pltpu.PrefetchScalarGridSpec(
        num_scalar_prefetch=1,
        grid=(Bd, n_pages // pg),
        in_specs=[pl.BlockSpec((None, rows, HEAD_DIM), per_seq),
                  pl.BlockSpec((None, n_new, pg * flat), lambda b, g, pt: (b, 0, g)),
                  pl.BlockSpec((None, n_new, seln.shape[2]), per_seq),
                  pl.BlockSpec((None, rows, HEAD_DIM), per_seq),
                  pl.BlockSpec((None, rows, HEAD_DIM), per_seq)]
                 + [kv_spec(p) for p in range(pg)] + [kv_spec(p) for p in range(pg)],
        out_specs=pl.BlockSpec((None, rows, HEAD_DIM), per_seq),
        scratch_shapes=[pltpu.VMEM((rows, 1), F32), pltpu.VMEM((rows, 1), F32),
                        pltpu.VMEM((rows, HEAD_DIM), F32)],
    )
    return pl.pallas_call(
        functools.partial(_sample_attn_kernel, pg=pg, n_new=n_new),
        grid_spec=grid_spec,
        out_shape=jax.ShapeDtypeStruct((Bd, rows, HEAD_DIM), BF16),
        compiler_params=_cparams("arbitrary", "arbitrary"),
        name="sample_attention",
    )(page_table, q_s, selx, seln, k_new, v_new, *([cache_k_l] * pg), *([cache_v_l] * pg))


def _out_proj_kernel(op_ref, os_ref, w_ref, xp_ref, xs_ref, y_ref, *, n_prompt_tiles):
    is_prompt = pl.program_id(0) < n_prompt_tiles
    o = jnp.where(is_prompt, op_ref[...], os_ref[...])
    x = jnp.where(is_prompt, xp_ref[...], xs_ref[...])
    y_ref[...] = x + _dot(o, w_ref[...])


def _out_proj(o_p, o_s, w, x_p, x_s, tm):
    n_p, D = x_p.shape
    T = n_p + x_s.shape[0]
    npt = n_p // tm
    prow = lambda i: (jnp.minimum(i, npt - 1), 0)
    srow = lambda i: (jnp.maximum(i - npt, 0), 0)
    return pl.pallas_call(
        functools.partial(_out_proj_kernel, n_prompt_tiles=npt),
        grid=(T // tm,),
        in_specs=[pl.BlockSpec((tm, o_p.shape[1]), prow), pl.BlockSpec((tm, o_s.shape[1]), srow),
                  pl.BlockSpec(w.shape, lambda i: (0, 0)),
                  pl.BlockSpec((tm, D), prow), pl.BlockSpec((tm, D), srow)],
        out_specs=pl.BlockSpec((tm, D), lambda i: (i, 0)),
        out_shape=jax.ShapeDtypeStruct((T, D), F32),
        compiler_params=_cparams("arbitrary"),
        name="attn_out_proj",
    )(o_p, o_s, w.astype(BF16), x_p, x_s)


def _ffn_kernel(x_ref, g_ref, wg_ref, wu_ref, wd_ref, y_ref, h_ref):
    f = pl.program_id(1)

    @pl.when(f == 0)
    def _():
        x = x_ref[...]
        h_ref[...] = _rms(x, g_ref[...]).astype(BF16)
        y_ref[...] = x

    h = h_ref[...]
    a = jax.nn.silu(_dot(h, wg_ref[...])) * _dot(h, wu_ref[...])
    y_ref[...] += _dot(a.astype(BF16), wd_ref[...])


def _ffn(x, g, w_gate, w_up, w_down, tm, tf):
    T, D = x.shape
    F = w_gate.shape[1]
    row = lambda i, f: (i, 0)
    return pl.pallas_call(
        _ffn_kernel,
        grid=(T // tm, F // tf),
        in_specs=[pl.BlockSpec((tm, D), row), pl.BlockSpec((1, D), lambda i, f: (0, 0)),
                  pl.BlockSpec((D, tf), lambda i, f: (0, f)), pl.BlockSpec((D, tf), lambda i, f: (0, f)),
                  pl.BlockSpec((tf, D), lambda i, f: (f, 0))],
        out_specs=pl.BlockSpec((tm, D), row),
        out_shape=jax.ShapeDtypeStruct((T, D), F32),
        scratch_shapes=[pltpu.VMEM((tm, D), BF16)],
        compiler_params=_cparams("arbitrary", "arbitrary"),
        name="dense_swiglu",
    )(x, g.reshape(1, D), w_gate.astype(BF16), w_up.astype(BF16), w_down.astype(BF16))


def _gmlp_kernel(x_ref, g_ref, win_ref, lng_ref, lnb_ref, ws_ref, bs_ref, wout_ref, y_ref, v_ref):
    tm, D = x_ref.shape
    W = lng_ref.shape[1]
    gd = W // GM_GROUPS
    x = x_ref[...]
    h = _rms(x, g_ref[...]).astype(BF16)
    z = _dot(h, win_ref[...])
    uv = 0.5 * z * (1.0 + lax.erf(z * math.sqrt(0.5)))
    u = uv[:, :W]
    v = uv[:, W:]
    mu = jnp.mean(v, axis=-1, keepdims=True)
    var = jnp.mean(jnp.square(v - mu), axis=-1, keepdims=True)
    v = (v - mu) * lax.rsqrt(var + EPS) * lng_ref[...] + lnb_ref[...]
    v_ref[...] = v
    vb = v.astype(BF16)
    ri = lax.broadcasted_iota(I32, (GM_CHUNK, GM_CHUNK), 0)
    ci = lax.broadcasted_iota(I32, (GM_CHUNK, GM_CHUNK), 1)
    causal = ci <= ri
    ws = [jnp.where(causal, ws_ref[g], 0.0).astype(BF16) for g in range(GM_GROUPS)]
    chunks = []
    for c in range(tm // GM_CHUNK):
        rs = slice(c * GM_CHUNK, (c + 1) * GM_CHUNK)
        mixed = jnp.concatenate(
            [_dot(ws[g], vb[rs, g * gd:(g + 1) * gd]) for g in range(GM_GROUPS)], axis=1)
        chunks.append(u[rs, :] * (mixed + bs_ref[...]))
    gated = jnp.concatenate(chunks, axis=0).astype(BF16)
    y_ref[...] = x + _dot(gated, wout_ref[...])


def _gmlp(x, g, w_in, ln_g, ln_b, w_s, b_s, w_out, tm, n_prompt_tiles, n_new):
    T, D = x.shape
    W = ln_g.shape[0]
    gd = W // GM_GROUPS
    eye = jnp.eye(GM_CHUNK // n_new, dtype=w_s.dtype)
    ws_sample = jax.vmap(lambda m: jnp.kron(eye, m))(w_s[:, :n_new, :n_new])
    ws_all = jnp.stack([w_s, ws_sample])
    bs_prompt = jnp.repeat(b_s.T, gd, axis=1)
    bs_sample = jnp.tile(bs_prompt[:n_new], (GM_CHUNK // n_new, 1))
    bs_all = jnp.stack([bs_prompt, bs_sample])
    kind = lambda i: jnp.where(i >= n_prompt_tiles, 1, 0)
    row = lambda i: (i, 0)
    const = lambda i: (0, 0)
    n_tiles = T // tm
    return pl.pallas_call(
        _gmlp_kernel,
        grid=(n_tiles,),
        in_specs=[pl.BlockSpec((tm, D), row), pl.BlockSpec((1, D), const),
                  pl.BlockSpec((D, 2 * W), const), pl.BlockSpec((1, W), const), pl.BlockSpec((1, W), const),
                  pl.BlockSpec((None, GM_GROUPS, GM_CHUNK, GM_CHUNK), lambda i: (kind(i), 0, 0, 0)),
                  pl.BlockSpec((None, GM_CHUNK, W), lambda i: (kind(i), 0, 0)),
                  pl.BlockSpec((W, D), const)],
        out_specs=[pl.BlockSpec((tm, D), row),
                   pl.BlockSpec((tm, W), lambda i: (jnp.maximum(i - n_prompt_tiles, 0), 0))],
        out_shape=[jax.ShapeDtypeStruct((T, D), F32),
                   jax.ShapeDtypeStruct(((n_tiles - n_prompt_tiles) * tm, W), F32)],
        compiler_params=_cparams("arbitrary"),
        name="gmlp_mixer",
    )(x, g.reshape(1, D), w_in.astype(BF16), ln_g.reshape(1, W), ln_b.reshape(1, W),
      ws_all, bs_all, w_out.astype(BF16))


def _router_kernel(x_ref, g_ref, wr_ref, h_ref, comb_ref, rank_ref, rankt_ref):
    tm = x_ref.shape[0]
    hf = _rms(x_ref[...], g_ref[...])
    h_ref[...] = hf.astype(BF16)
    logits = jnp.dot(hf, wr_ref[...], preferred_element_type=F32, precision=lax.Precision.HIGHEST)
    lane = lax.broadcasted_iota(I32, logits.shape, 1)
    logits = jnp.where(lane < N_EXPERTS, logits, -jnp.inf)
    m1 = jnp.max(logits, axis=-1, keepdims=True)
    i1 = jnp.min(jnp.where(logits == m1, lane, LANES), axis=-1, keepdims=True)
    rest = jnp.where(lane == i1, -jnp.inf, logits)
    m2 = jnp.max(rest, axis=-1, keepdims=True)
    i2 = jnp.min(jnp.where(rest == m2, lane, LANES), axis=-1, keepdims=True)
    e2 = jnp.exp(m2 - m1)
    den = 1.0 + e2
    comb_ref[...] = jnp.where(lane == i1, 1.0 / den, 0.0) + jnp.where(lane == i2, e2 / den, 0.0)
    routed = jnp.where((lane == i1) | (lane == i2), 1.0, 0.0).astype(BF16)
    r = lax.broadcasted_iota(I32, (tm, tm), 0)
    c = lax.broadcasted_iota(I32, (tm, tm), 1)
    rank = _dot(jnp.where(c < r, 1.0, 0.0).astype(BF16), routed)
    rank = jnp.where(routed > 0, rank, -1.0)
    rank_ref[...] = rank
    rankt_ref[...] = rank.T


def _router(x, g, w_router, tm):
    T, D = x.shape
    wr = jnp.pad(w_router, ((0, 0), (0, LANES - N_EXPERTS)))
    row = lambda i: (i, 0)
    colb = lambda i: (0, i)
    const = lambda i: (0, 0)
    return pl.pallas_call(
        _router_kernel,
        grid=(T // tm,),
        in_specs=[pl.BlockSpec((tm, D), row), pl.BlockSpec((1, D), const), pl.BlockSpec((D, LANES), const)],
        out_specs=[pl.BlockSpec((tm, D), row), pl.BlockSpec((tm, LANES), row),
                   pl.BlockSpec((tm, LANES), row), pl.BlockSpec((LANES, tm), colb)],
        out_shape=[jax.ShapeDtypeStruct((T, D), BF16), jax.ShapeDtypeStruct((T, LANES), F32),
                   jax.ShapeDtypeStruct((T, LANES), F32), jax.ShapeDtypeStruct((LANES, T), F32)],
        compiler_params=_cparams("arbitrary"),
        name="moe_router",
    )(x, g.reshape(1, D), wr)


def _moe_kernel(cnt_ref, h_ref, comb_ref, rank_ref, rankt_ref, wg_ref, wu_ref, wd_ref, acc_ref, *outs,
                e, n_prompt_tiles):
    i = pl.program_id(0)
    tm = h_ref.shape[0]
    gate = comb_ref[:, e:e + 1]
    rank_c = rank_ref[:, e:e + 1]
    rank_r = rankt_ref[e:e + 1, :]
    n_blocks = (cnt_ref[i] + MOE_ROWS - 1) // MOE_ROWS

    y_ref = outs[0] if len(outs) == 1 else outs[2]

    def block(jb, carry):
        base = (jb * MOE_ROWS).astype(F32)
        slot_r = lax.broadcasted_iota(I32, (MOE_ROWS, tm), 0).astype(F32) + base
        take = jnp.where(rank_r == slot_r, 1.0, 0.0).astype(BF16)
        xs = _dot(take, h_ref[...]).astype(BF16)
        a = jax.nn.silu(_dot(xs, wg_ref[...])) * _dot(xs, wu_ref[...])
        y = _dot(a.astype(BF16), wd_ref[...])
        slot_c = lax.broadcasted_iota(I32, (tm, MOE_ROWS), 1).astype(F32) + base
        put = jnp.where(rank_c == slot_c, 1.0, 0.0).astype(BF16)
        y_ref[...] += gate * _dot(put, y.astype(BF16))
        return carry

    y_ref[...] = acc_ref[...]
    lax.fori_loop(0, n_blocks, block, 0)
    if len(outs) > 1:
        @pl.when(i < n_prompt_tiles)
        def _():
            outs[0][...] = y_ref[...]

        @pl.when(i >= n_prompt_tiles)
        def _():
            outs[1][...] = y_ref[...]


def _moe_expert(e, acc, h, comb, rank, rankt, counts_e, w_gate, w_up, w_down, tm, n_prompt_tiles, last):
    T, D = acc.shape
    F = w_gate.shape[2]
    tok = lambda i, cnt: (i, 0)
    once = pl.Buffered(1)
    weight = lambda shape: pl.BlockSpec((None,) + shape, lambda i, cnt: (e, 0, 0), pipeline_mode=once)
    if last:
        npt = n_prompt_tiles
        out_specs = [pl.BlockSpec((tm, D), lambda i, cnt: (jnp.minimum(i, npt - 1), 0)),
                     pl.BlockSpec((tm, D), lambda i, cnt: (jnp.maximum(i - npt, 0), 0))]
        out_shape = [jax.ShapeDtypeStruct((npt * tm, D), F32), jax.ShapeDtypeStruct((T - npt * tm, D), F32)]
        aliases = {}
    else:
        out_specs = pl.BlockSpec((tm, D), tok)
        out_shape = jax.ShapeDtypeStruct((T, D), F32)
        aliases = {8: 0}
    grid_spec = pltpu.PrefetchScalarGridSpec(
        num_scalar_prefetch=1,
        grid=(T // tm,),
        in_specs=[pl.BlockSpec((tm, D), tok), pl.BlockSpec((tm, LANES), tok), pl.BlockSpec((tm, LANES), tok),
                  pl.BlockSpec((LANES, tm), lambda i, cnt: (0, i)),
                  weight((D, F)), weight((D, F)), weight((F, D)),
                  pl.BlockSpec((tm, D), tok)],
        out_specs=out_specs,
        scratch_shapes=[pltpu.VMEM((tm, D), F32)] if last else [],
    )
    return pl.pallas_call(
        functools.partial(_moe_kernel, e=e, n_prompt_tiles=n_prompt_tiles),
        grid_spec=grid_spec,
        out_shape=out_shape,
        input_output_aliases=aliases,
        compiler_params=_cparams("arbitrary"),
        name=f"moe_expert_{e}",
    )(counts_e, h, comb, rank, rankt, w_gate, w_up, w_down, acc)


def _token_tile(n_prompt, n_sample):
    for tm in (512, 256, 128):
        if n_prompt % tm == 0 and n_sample % tm == 0:
            return tm
    raise ValueError("token counts must be multiples of 128")


def _largest_divisor(n, candidates):
    for c in candidates:
        if n % c == 0:
            return c
    return n


def kernel(x_prompt, x_sample, cache_k, cache_v, cache_kidx, page_table, norm_mix, norm_ffn, attn_w_in, attn_g_q, attn_g_k, attn_g_kidx, attn_w_out, gm_w_in, gm_ln_g, gm_ln_b, gm_w_s, gm_b_s, gm_w_out, ffn_w_gate, ffn_w_up, ffn_w_down, moe_w_router, moe_w_gate, moe_w_up, moe_w_down):
    B, S, D = x_prompt.shape
    Bd, n_new, _ = x_sample.shape
    n_pages = page_table.shape[1]
    past = n_pages * PAGE_SIZE
    assert norm_mix.shape[0] == 2 and D == N_HEADS * HEAD_DIM
    n_p, n_s = B * S, Bd * n_new
    T = n_p + n_s
    tm = _token_tile(n_p, n_s)
    npt = n_p // tm
    topk_prompt = min(IDX_TOPK, S // 4)
    topk_sample = min(IDX_TOPK, (past + n_new) // 4)
    d = N_HEADS * HEAD_DIM
    x_p = x_prompt.reshape(n_p, D)
    x_s = x_sample.reshape(n_s, D)

    q, kb, vb, qi, kw, k_p, v_p, k_s, v_s = _attn_project(
        x_p, x_s, norm_mix[0], attn_w_in[0], attn_g_q[0], attn_g_k[0], attn_g_kidx[0], tm)
    tq = _largest_divisor(S, (256, 128, 64))
    o_p = _prompt_attention(q, qi, kw, kb, vb, B, S, tq, topk_prompt)

    C = past + LANES
    qi_s = qi[n_p:].reshape(Bd, n_new * IDX_HEADS, IDX_DIM)
    wi_s = kw[n_p:, IDX_DIM:IDX_DIM + IDX_HEADS].reshape(Bd, n_new * IDX_HEADS, 1)
    ki_new_t = jnp.pad(jnp.swapaxes(kw[n_p:, :IDX_DIM].reshape(Bd, n_new, IDX_DIM), 1, 2),
                       ((0, 0), (0, 0), (0, LANES - n_new)))
    scores = _sample_scores(page_table, qi_s, wi_s, ki_new_t, jnp.swapaxes(cache_kidx[0], 1, 2), C)
    selx, seln = _sample_topk(scores.reshape(n_s, C), topk_sample, past, n_new,
                              _largest_divisor(n_s, (128, 64, 32, 16)))
    flat = PAGE_SIZE * N_HEADS
    n_pool = cache_k.shape[1]
    pg = _largest_divisor(n_pages, tuple(range(MAX_PAGES_PER_STEP, 0, -1)))
    o_s = _sample_attention(
        page_table, q[n_p:].reshape(Bd, n_new * N_HEADS, HEAD_DIM),
        selx.reshape(Bd, n_new, past * N_HEADS), seln.reshape(Bd, n_new, LANES),
        k_s.reshape(Bd, n_new * N_HEADS, HEAD_DIM), v_s.reshape(Bd, n_new * N_HEADS, HEAD_DIM),
        cache_k[0].reshape(n_pool, flat, HEAD_DIM), cache_v[0].reshape(n_pool, flat, HEAD_DIM), pg)
    x = _out_proj(o_p, o_s.reshape(n_s, d), attn_w_out[0], x_p, x_s, tm)
    x = _ffn(x, norm_ffn[0], ffn_w_gate[0], ffn_w_up[0], ffn_w_down[0], tm,
             _largest_divisor(ffn_w_gate.shape[2], (1408, 1024, 512, 256, 128)))

    x, gm_v = _gmlp(x, norm_mix[1], gm_w_in[0], gm_ln_g[0], gm_ln_b[0], gm_w_s[0], gm_b_s[0], gm_w_out[0],
                    tm, npt, n_new)
    h, comb, rank, rankt = _router(x, norm_ffn[1], moe_w_router[0], tm)
    counts = jnp.sum((rankt[:N_EXPERTS] >= 0.0).reshape(N_EXPERTS, T // tm, tm), axis=-1).astype(I32)
    wg, wu, wd = moe_w_gate[0].astype(BF16), moe_w_up[0].astype(BF16), moe_w_down[0].astype(BF16)
    for e in range(N_EXPERTS):
        x = _moe_expert(e, x, h, comb, rank, rankt, counts[e], wg, wu, wd, tm, npt, e == N_EXPERTS - 1)
    y_p, y_s = x

    n_pp = n_p // PAGE_SIZE
    return (y_p.reshape(B, S, D), y_s.reshape(Bd, n_new, D),
            k_p.reshape(1, n_pp, PAGE_SIZE, N_HEADS, HEAD_DIM),
            v_p.reshape(1, n_pp, PAGE_SIZE, N_HEADS, HEAD_DIM),
            kw[:n_p, :IDX_DIM].reshape(1, n_pp, PAGE_SIZE, IDX_DIM),
            k_s.reshape(1, Bd, n_new, N_HEADS, HEAD_DIM),
            v_s.reshape(1, Bd, n_new, N_HEADS, HEAD_DIM),
            kw[n_p:, :IDX_DIM].reshape(1, Bd, n_new, IDX_DIM),
            gm_v.reshape(1, Bd, n_new, gm_ln_g.shape[1]))
```

```python
import functools
import math

import jax
import jax.numpy as jnp
from jax import lax
from jax.experimental import pallas as pl
from jax.experimental.pallas import tpu as pltpu

F32 = jnp.float32
BF16 = jnp.bfloat16
I32 = jnp.int32

N_HEADS = 8
HEAD_DIM = 128
IDX_HEADS = 8
IDX_DIM = 64
IDX_TOPK = 256
PAGE_SIZE = 128
GM_CHUNK = 128
GM_GROUPS = 8
N_EXPERTS = 8
EPS = 1e-6
NEG = -1e30

LANES = 128
MOE_ROWS = 144
MAX_PAGES_PER_STEP = 16
VMEM_LIMIT = 56 * 1024 * 1024


def _cparams(*sem):
    return pltpu.CompilerParams(dimension_semantics=sem, vmem_limit_bytes=VMEM_LIMIT)


def _dot(a, b):
    return jnp.dot(a, b, preferred_element_type=F32)


def _dot_nt(a, b):
    return lax.dot_general(a, b, (((1,), (1,)), ((), ())), preferred_element_type=F32)


def _rms(x, g):
    return x * lax.rsqrt(jnp.mean(x * x, axis=-1, keepdims=True) + EPS) * g


def _count(m):
    return jnp.sum(jnp.where(m, 1.0, 0.0), axis=1, keepdims=True)


def _topk_select(x_ref, valid, k):
    R, C = x_ref.shape
    kf = jnp.float32(k)
    col = lax.broadcasted_iota(I32, (R, C), 1)
    x = x_ref[...]
    lo = jnp.min(x, axis=1, keepdims=True)
    hi = jnp.max(x, axis=1, keepdims=True)
    cnt = jnp.full((R, 1), float(C), F32)

    def probe(state, pivot):
        lo, hi, cnt = state
        c = _count(x_ref[...] >= pivot)
        up = (c >= kf) & (pivot > lo)
        dn = (c < kf) & (pivot < hi)
        return jnp.where(up, pivot, lo), jnp.where(dn, pivot, hi), jnp.where(up, c, cnt)

    state = probe((lo, hi, cnt), hi)
    state = probe(state, jnp.full((R, 1), NEG * (1.0 - 2.0 ** -20), F32))
    state = probe(state, jnp.min(jnp.where(x > NEG, x, jnp.inf), axis=1, keepdims=True))
    state = probe(state, jnp.zeros((R, 1), F32))
    state = probe(state, jnp.full((R, 1), jnp.finfo(F32).tiny, F32))

    def midpoint(lo, hi):
        return 0.5 * lo + 0.5 * hi

    def n_open(state):
        lo, hi, cnt = state
        mid = midpoint(lo, hi)
        return jnp.max(jnp.where((cnt > kf) & (mid > lo) & (mid < hi), 1.0, 0.0))

    def cond(carry):
        it, open_rows, _ = carry
        return (open_rows > 0.0) & (it < 512)

    def body(carry):
        it, _, state = carry
        state = probe(state, midpoint(state[0], state[1]))
        return it + 1, n_open(state), state

    _, _, (tau, _, _) = lax.while_loop(cond, body, (jnp.int32(0), n_open(state), state))
    gt = x > tau
    eq = x == tau
    need = kf - _count(gt)
    excess = _count(eq & valid) - need
    n_steps = jnp.where(jnp.max(excess) > 0.0, C.bit_length(), 0)

    def col_step(_, lohi):
        lo, hi = lohi
        mid = (lo + hi) >> 1
        ok = _count((x_ref[...] == tau) & (col <= mid)) >= need
        return jnp.where(ok, lo, mid), jnp.where(ok, mid, hi)

    lo0 = jnp.full((R, 1), -1, I32)
    hi0 = jnp.full((R, 1), C - 1, I32)
    _, last = lax.fori_loop(0, n_steps, col_step, (lo0, hi0))
    return (gt | (eq & (col <= last))) & valid


def _attn_proj_kernel(xp_ref, xs_ref, g_ref, w_ref, gq_ref, gk_ref, gki_ref,
                      q_ref, kb_ref, vb_ref, qi_ref, kw_ref, kp_ref, vp_ref, ks_ref, vs_ref, *, n_prompt_tiles):
    d = N_HEADS * HEAD_DIM
    tm = xp_ref.shape[0]
    is_prompt = pl.program_id(0) < n_prompt_tiles
    x = jnp.where(is_prompt, xp_ref[...], xs_ref[...])
    h = _rms(x, g_ref[...]).astype(BF16)
    scale = HEAD_DIM ** -0.5

    def head_norm(y, g):
        return y * lax.rsqrt(jnp.mean(y * y, axis=-1, keepdims=True) + EPS) * g

    def put_heads(y, p_ref, s_ref):
        @pl.when(is_prompt)
        def _():
            for hd in range(N_HEADS):
                p_ref[pl.ds(hd, tm, stride=N_HEADS), :] = y[hd]

        @pl.when(jnp.logical_not(is_prompt))
        def _():
            for hd in range(N_HEADS):
                s_ref[pl.ds(hd, tm, stride=N_HEADS), :] = y[hd]

    yq = _dot(h, w_ref[:, 0:d])
    for hd in range(N_HEADS):
        sl = slice(hd * HEAD_DIM, (hd + 1) * HEAD_DIM)
        q_ref[:, sl] = (head_norm(yq[:, sl], gq_ref[...]) * scale).astype(BF16)
    yk = _dot(h, w_ref[:, d:2 * d])
    kn = [head_norm(yk[:, hd * HEAD_DIM:(hd + 1) * HEAD_DIM], gk_ref[...]) for hd in range(N_HEADS)]
    put_heads(kn, kp_ref, ks_ref)
    yv = _dot(h, w_ref[:, 2 * d:3 * d])

    @pl.when(is_prompt)
    def _():
        for hd in range(N_HEADS):
            kb_ref[:, hd * HEAD_DIM:(hd + 1) * HEAD_DIM] = kn[hd].astype(BF16)
        vb_ref[...] = yv.astype(BF16)

    put_heads([yv[:, hd * HEAD_DIM:(hd + 1) * HEAD_DIM] for hd in range(N_HEADS)], vp_ref, vs_ref)
    ni = IDX_HEADS * IDX_DIM
    qi_ref[...] = (_dot(h, w_ref[:, 3 * d:3 * d + ni]) * (IDX_DIM ** -0.5)).astype(BF16)
    yt = _dot(h, w_ref[:, 3 * d + ni:3 * d + ni + LANES])
    lane = lax.broadcasted_iota(I32, yt.shape, 1)
    is_ki = lane < IDX_DIM
    ms = jnp.sum(jnp.where(is_ki, yt * yt, 0.0), axis=-1, keepdims=True) * (1.0 / IDX_DIM)
    kw_ref[...] = jnp.where(is_ki, yt * lax.rsqrt(ms + EPS) * gki_ref[...], yt * (IDX_HEADS ** -0.5))


def _attn_project(x_p, x_s, g_mix, w_in, g_q, g_k, g_kidx, tm):
    n_p, D = x_p.shape
    n_s = x_s.shape[0]
    T = n_p + n_s
    npt = n_p // tm
    d = N_HEADS * HEAD_DIM
    n_in = w_in.shape[1]
    n_pad = 3 * d + IDX_HEADS * IDX_DIM + LANES
    w = jnp.pad(w_in, ((0, 0), (0, n_pad - n_in))).astype(BF16)
    gki = jnp.pad(g_kidx, (0, LANES - IDX_DIM)).reshape(1, LANES)
    row = lambda i: (i, 0)
    prow = lambda i: (jnp.minimum(i, npt - 1), 0)
    srow = lambda i: (jnp.maximum(i - npt, 0), 0)
    const = lambda i: (0, 0)
    stream = lambda n, dt: jax.ShapeDtypeStruct((T, n), dt)
    return pl.pallas_call(
        functools.partial(_attn_proj_kernel, n_prompt_tiles=npt),
        grid=(T // tm,),
        in_specs=[pl.BlockSpec((tm, D), prow), pl.BlockSpec((tm, D), srow), pl.BlockSpec((1, D), const),
                  pl.BlockSpec((D, n_pad), const), pl.BlockSpec((1, HEAD_DIM), const),
                  pl.BlockSpec((1, HEAD_DIM), const), pl.BlockSpec((1, LANES), const)],
        out_specs=[pl.BlockSpec((tm, d), row), pl.BlockSpec((tm, d), prow), pl.BlockSpec((tm, d), prow),
                   pl.BlockSpec((tm, IDX_HEADS * IDX_DIM), row), pl.BlockSpec((tm, LANES), row),
                   pl.BlockSpec((tm * N_HEADS, HEAD_DIM), prow), pl.BlockSpec((tm * N_HEADS, HEAD_DIM), prow),
                   pl.BlockSpec((tm * N_HEADS, HEAD_DIM), srow), pl.BlockSpec((tm * N_HEADS, HEAD_DIM), srow)],
        out_shape=[stream(d, BF16), jax.ShapeDtypeStruct((n_p, d), BF16), jax.ShapeDtypeStruct((n_p, d), BF16),
                   stream(IDX_HEADS * IDX_DIM, BF16), stream(LANES, F32),
                   jax.ShapeDtypeStruct((n_p * N_HEADS, HEAD_DIM), F32),
                   jax.ShapeDtypeStruct((n_p * N_HEADS, HEAD_DIM), F32),
                   jax.ShapeDtypeStruct((n_s * N_HEADS, HEAD_DIM), F32),
                   jax.ShapeDtypeStruct((n_s * N_HEADS, HEAD_DIM), F32)],
        compiler_params=_cparams("arbitrary"),
        name="attn_project",
    )(x_p, x_s, g_mix.reshape(1, D), w, g_q.reshape(1, HEAD_DIM), g_k.reshape(1, HEAD_DIM), gki)


def _prompt_attn_kernel(*refs, topk, t_first, aliased):
    if aliased:
        refs = refs[1:]
    q_ref, qi_ref, kwq_ref, kwk_ref, kb_ref, vb_ref, o_ref, x_ref, sel_ref = refs
    tq, S = x_ref.shape
    t0 = t_first + pl.program_id(1) * tq
    ki = kwk_ref[:, 0:IDX_DIM].astype(BF16)
    qi = qi_ref[...]
    score = None
    for h in range(IDX_HEADS):
        logit = _dot_nt(qi[:, h * IDX_DIM:(h + 1) * IDX_DIM], ki)
        term = jnp.maximum(logit, 0.0) * kwq_ref[:, IDX_DIM + h:IDX_DIM + h + 1]
        score = term if score is None else score + term
    row = lax.broadcasted_iota(I32, (tq, S), 0) + t0
    col = lax.broadcasted_iota(I32, (tq, S), 1)
    valid = col <= row
    x_ref[...] = jnp.where(valid, score, NEG)
    sel_ref[...] = jnp.where(_topk_select(x_ref, valid, topk), 1.0, 0.0)
    for h in range(N_HEADS):
        sl = slice(h * HEAD_DIM, (h + 1) * HEAD_DIM)
        s = _dot_nt(q_ref[:, sl], kb_ref[:, sl])
        s = jnp.where(sel_ref[...] != 0.0, s, NEG)
        m = jnp.max(s, axis=-1, keepdims=True)
        p = jnp.exp(s - m)
        l = jnp.sum(p, axis=-1, keepdims=True)
        o = _dot(p.astype(BF16), vb_ref[:, sl]) / l
        o_ref[:, sl] = o.astype(o_ref.dtype)


def _prompt_attention(q, qi, kw, kb, vb, B, S, tq, topk):
    d = N_HEADS * HEAD_DIM
    nq = S // tq
    span = 2 if nq % 2 == 0 else 1
    kw3, kb3, vb3 = (a[:B * S].reshape(B, S, a.shape[1]) for a in (kw, kb, vb))
    o = None
    for j in range(nq // span):
        s_eff = (j + 1) * span * tq
        qrow = lambda b, i, j=j: (b * nq + j * span + i, 0)
        batch = lambda b, i: (b, 0, 0)
        in_specs = [pl.BlockSpec((tq, d), qrow), pl.BlockSpec((tq, IDX_HEADS * IDX_DIM), qrow),
                    pl.BlockSpec((tq, LANES), qrow), pl.BlockSpec((None, s_eff, LANES), batch),
                    pl.BlockSpec((None, s_eff, d), batch), pl.BlockSpec((None, s_eff, d), batch)]
        args = [q, qi, kw, kw3, kb3, vb3]
        aliases = {}
        if o is not None:
            in_specs = [pl.BlockSpec(memory_space=pl.ANY)] + in_specs
            args = [o] + args
            aliases = {0: 0}
        o = pl.pallas_call(
            functools.partial(_prompt_attn_kernel, topk=topk, t_first=j * span * tq, aliased=o is not None),
            grid=(B, span),
            in_specs=in_specs,
            out_specs=pl.BlockSpec((tq, d), qrow),
            out_shape=jax.ShapeDtypeStruct((B * S, d), BF16),
            scratch_shapes=[pltpu.VMEM((tq, s_eff), F32), pltpu.VMEM((tq, s_eff), F32)],
            input_output_aliases=aliases,
            compiler_params=_cparams("arbitrary", "arbitrary"),
            name=f"prompt_attention_{j}",
        )(*args)
    return o


def _sample_scores_kernel(pt_ref, qi_ref, w_ref, kin_ref, *rest, n_pages, n_new):
    page_refs = rest[:n_pages]
    s_ref = rest[n_pages]
    kall_ref = rest[n_pages + 1]
    past = n_pages * PAGE_SIZE
    C = kall_ref.shape[1]
    for p in range(n_pages):
        kall_ref[:, p * PAGE_SIZE:(p + 1) * PAGE_SIZE] = page_refs[p][...].astype(BF16)
    kall_ref[:, past:C] = kin_ref[...].astype(BF16)
    logit = _dot(qi_ref[...], kall_ref[...])
    term = jnp.maximum(logit, 0.0) * w_ref[...]
    col = lax.broadcasted_iota(I32, (1, C), 1)
    for t in range(n_new):
        s_t = jnp.sum(term[t * IDX_HEADS:(t + 1) * IDX_HEADS, :], axis=0, keepdims=True)
        s_ref[t:t + 1, :] = jnp.where(col <= past + t, s_t, NEG)


def _sample_scores(page_table, qi_s, wi_s, ki_new_t, kidx_t, C):
    Bd, n_pages = page_table.shape
    n_new = qi_s.shape[1] // IDX_HEADS
    page_spec = lambda p: pl.BlockSpec((None, IDX_DIM, PAGE_SIZE), lambda b, pt: (pt[b, p], 0, 0))
    per_seq = lambda b, pt: (b, 0, 0)
    grid_spec = pltpu.PrefetchScalarGridSpec(
        num_scalar_prefetch=1,
        grid=(Bd,),
        in_specs=[pl.BlockSpec((None, n_new * IDX_HEADS, IDX_DIM), per_seq),
                  pl.BlockSpec((None, n_new * IDX_HEADS, 1), per_seq),
                  pl.BlockSpec((None, IDX_DIM, C - n_pages * PAGE_SIZE), per_seq)]
                 + [page_spec(p) for p in range(n_pages)],
        out_specs=pl.BlockSpec((None, n_new, C), per_seq),
        scratch_shapes=[pltpu.VMEM((IDX_DIM, C), BF16)],
    )
    return pl.pallas_call(
        functools.partial(_sample_scores_kernel, n_pages=n_pages, n_new=n_new),
        grid_spec=grid_spec,
        out_shape=jax.ShapeDtypeStruct((Bd, n_new, C), F32),
        compiler_params=_cparams("arbitrary"),
        name="sample_scores",
    )(page_table, qi_s, wi_s, ki_new_t, *([kidx_t] * n_pages))


def _sample_topk_kernel(s_ref, selx_ref, seln_ref, x_ref, *, topk, past, n_new):
    R, C = x_ref.shape
    flat = PAGE_SIZE * N_HEADS
    row = lax.broadcasted_iota(I32, (R, C), 0)
    col = lax.broadcasted_iota(I32, (R, C), 1)
    valid = col <= past + (row % n_new)
    x_ref[...] = s_ref[...]
    sel = jnp.where(_topk_select(x_ref, valid, topk), 1.0, 0.0).astype(BF16)
    e_r = lax.broadcasted_iota(I32, (PAGE_SIZE, flat), 0)
    e_c = lax.broadcasted_iota(I32, (PAGE_SIZE, flat), 1)
    spread = jnp.where(e_c // N_HEADS == e_r, 1.0, 0.0).astype(BF16)
    for p in range(past // PAGE_SIZE):
        selx_ref[:, p * flat:(p + 1) * flat] = _dot(
            sel[:, p * PAGE_SIZE:(p + 1) * PAGE_SIZE], spread).astype(selx_ref.dtype)
    seln_ref[...] = sel[:, past:C]


def _sample_topk(scores, topk, past, n_new, tr):
    R, C = scores.shape
    wide = past * N_HEADS
    row = lambda i: (i, 0)
    return pl.pallas_call(
        functools.partial(_sample_topk_kernel, topk=topk, past=past, n_new=n_new),
        grid=(R // tr,),
        in_specs=[pl.BlockSpec((tr, C), row)],
        out_specs=[pl.BlockSpec((tr, wide), row), pl.BlockSpec((tr, C - past), row)],
        out_shape=[jax.ShapeDtypeStruct((R, wide), BF16), jax.ShapeDtypeStruct((R, C - past), BF16)],
        scratch_shapes=[pltpu.VMEM((tr, C), F32)],
        compiler_params=_cparams("arbitrary"),
        name="sample_topk",
    )(scores)


def _sample_attn_kernel(pt_ref, q_ref, selx_ref, seln_ref, kn_ref, vn_ref, *rest, pg, n_new):
    k_refs = rest[:pg]
    v_refs = rest[pg:2 * pg]
    o_ref, m_ref, l_ref, acc_ref = rest[2 * pg:]
    g = pl.program_id(1)
    rows = n_new * N_HEADS
    flat = PAGE_SIZE * N_HEADS

    @pl.when(g == 0)
    def _():
        m_ref[...] = jnp.full(m_ref.shape, NEG, F32)
        l_ref[...] = jnp.zeros(l_ref.shape, F32)
        acc_ref[...] = jnp.zeros(acc_ref.shape, F32)

    q = q_ref[...]

    def expand_rows(x):
        return jnp.concatenate(
            [jnp.broadcast_to(x[t:t + 1, :], (N_HEADS, x.shape[1])) for t in range(n_new)], axis=0)

    def update(scores, keeps, values):
        scores = [jnp.where(kp, s, NEG) for s, kp in zip(scores, keeps)]
        m_old = m_ref[...]
        m_new = m_old
        for s in scores:
            m_new = jnp.maximum(m_new, jnp.max(s, axis=-1, keepdims=True))
        alpha = jnp.exp(m_old - m_new)
        l_new = alpha * l_ref[...]
        acc = alpha * acc_ref[...]
        for s, kp, v in zip(scores, keeps, values):
            p = jnp.where(kp, jnp.exp(s - m_new), 0.0)
            l_new = l_new + jnp.sum(p, axis=-1, keepdims=True)
            acc = acc + _dot(p.astype(BF16), v)
        m_ref[...] = m_new
        l_ref[...] = l_new
        acc_ref[...] = acc

    hrow = lax.broadcasted_iota(I32, (rows, flat), 0) % N_HEADS
    hcol = lax.broadcasted_iota(I32, (rows, flat), 1) % N_HEADS
    same_head = hrow == hcol
    scores, keeps, values = [], [], []
    for p in range(pg):
        scores.append(_dot_nt(q, k_refs[p][...].astype(BF16)))
        sel_p = selx_ref[:, p * flat:(p + 1) * flat].astype(F32)
        keeps.append(same_head & (expand_rows(sel_p) > 0.5))
        values.append(v_refs[p][...].astype(BF16))
    update(scores, keeps, values)

    @pl.when(g == pl.num_programs(1) - 1)
    def _():
        s = _dot_nt(q, kn_ref[...].astype(BF16))
        r2 = lax.broadcasted_iota(I32, (rows, rows), 0)
        c2 = lax.broadcasted_iota(I32, (rows, rows), 1)
        sn = seln_ref[...].astype(F32)
        seln = jnp.concatenate(
            [jnp.broadcast_to(sn[:, j:j + 1], (n_new, N_HEADS)) for j in range(n_new)], axis=1)
        keep = (r2 % N_HEADS == c2 % N_HEADS) & (expand_rows(seln) > 0.5)
        update([s], [keep], [vn_ref[...].astype(BF16)])
        o_ref[...] = (acc_ref[...] / l_ref[...]).astype(o_ref.dtype)


def _sample_attention(page_table, q_s, selx, seln, k_new, v_new, cache_k_l, cache_v_l, pg):
    Bd, n_pages = page_table.shape
    rows = q_s.shape[1]
    n_new = rows // N_HEADS
    flat = PAGE_SIZE * N_HEADS
    per_seq = lambda b, g, pt: (b, 0, 0)
    kv_spec = lambda p: pl.BlockSpec((None, flat, HEAD_DIM), lambda b, g, pt: (pt[b, g * pg + p], 0, 0))
    grid_spec = pltpu.PrefetchScalarGridSpec(
        num_scalar_prefetch=1,
        grid=(Bd, n_pages // pg),
        in_specs=[pl.BlockSpec((None, rows, HEAD_DIM), per_seq),
                  pl.BlockSpec((None, n_new, pg * flat), lambda b, g, pt: (b, 0, g)),
                  pl.BlockSpec((None, n_new, seln.shape[2]), per_seq),
                  pl.BlockSpec((None, rows, HEAD_DIM), per_seq),
                  pl.BlockSpec((None, rows, HEAD_DIM), per_seq)]
                 + [kv_spec(p) for p in range(pg)] + [kv_spec(p) for p in range(pg)],
        out_specs=pl.BlockSpec((None, rows, HEAD_DIM), per_seq),
        scratch_shapes=[pltpu.VMEM((rows, 1), F32), pltpu.VMEM((rows, 1), F32),
                        pltpu.VMEM((rows, HEAD_DIM), F32)],
    )
    return pl.pallas_call(
        functools.partial(_sample_attn_kernel, pg=pg, n_new=n_new),
        grid_spec=grid_spec,
        out_shape=jax.ShapeDtypeStruct((Bd, rows, HEAD_DIM), BF16),
        compiler_params=_cparams("arbitrary", "arbitrary"),
        name="sample_attention",
    )(page_table, q_s, selx, seln, k_new, v_new, *([cache_k_l] * pg), *([cache_v_l] * pg))


def _out_proj_kernel(op_ref, os_ref, w_ref, xp_ref, xs_ref, y_ref, *, n_prompt_tiles):
    is_prompt = pl.program_id(0) < n_prompt_tiles
    o = jnp.where(is_prompt, op_ref[...], os_ref[...])
    x = jnp.where(is_prompt, xp_ref[...], xs_ref[...])
    y_ref[...] = x + _dot(o, w_ref[...])


def _out_proj(o_p, o_s, w, x_p, x_s, tm):
    n_p, D = x_p.shape
    T = n_p + x_s.shape[0]
    npt = n_p // tm
    prow = lambda i: (jnp.minimum(i, npt - 1), 0)
    srow = lambda i: (jnp.maximum(i - npt, 0), 0)
    return pl.pallas_call(
        functools.partial(_out_proj_kernel, n_prompt_tiles=npt),
        grid=(T // tm,),
        in_specs=[pl.BlockSpec((tm, o_p.shape[1]), prow), pl.BlockSpec((tm, o_s.shape[1]), srow),
                  pl.BlockSpec(w.shape, lambda i: (0, 0)),
                  pl.BlockSpec((tm, D), prow), pl.BlockSpec((tm, D), srow)],
        out_specs=pl.BlockSpec((tm, D), lambda i: (i, 0)),
        out_shape=jax.ShapeDtypeStruct((T, D), F32),
        compiler_params=_cparams("arbitrary"),
        name="attn_out_proj",
    )(o_p, o_s, w.astype(BF16), x_p, x_s)


def _ffn_kernel(x_ref, g_ref, wg_ref, wu_ref, wd_ref, y_ref, h_ref):
    f = pl.program_id(1)

    @pl.when(f == 0)
    def _():
        x = x_ref[...]
        h_ref[...] = _rms(x, g_ref[...]).astype(BF16)
        y_ref[...] = x

    h = h_ref[...]
    a = jax.nn.silu(_dot(h, wg_ref[...])) * _dot(h, wu_ref[...])
    y_ref[...] += _dot(a.astype(BF16), wd_ref[...])


def _ffn(x, g, w_gate, w_up, w_down, tm, tf):
    T, D = x.shape
    F = w_gate.shape[1]
    row = lambda i, f: (i, 0)
    return pl.pallas_call(
        _ffn_kernel,
        grid=(T // tm, F // tf),
        in_specs=[pl.BlockSpec((tm, D), row), pl.BlockSpec((1, D), lambda i, f: (0, 0)),
                  pl.BlockSpec((D, tf), lambda i, f: (0, f)), pl.BlockSpec((D, tf), lambda i, f: (0, f)),
                  pl.BlockSpec((tf, D), lambda i, f: (f, 0))],
        out_specs=pl.BlockSpec((tm, D), row),
        out_shape=jax.ShapeDtypeStruct((T, D), F32),
        scratch_shapes=[pltpu.VMEM((tm, D), BF16)],
        compiler_params=_cparams("arbitrary", "arbitrary"),
        name="dense_swiglu",
    )(x, g.reshape(1, D), w_gate.astype(BF16), w_up.astype(BF16), w_down.astype(BF16))


def _gmlp_kernel(x_ref, g_ref, win_ref, lng_ref, lnb_ref, ws_ref, bs_ref, wout_ref, y_ref, v_ref):
    tm, D = x_ref.shape
    W = lng_ref.shape[1]
    gd = W // GM_GROUPS
    x = x_ref[...]
    h = _rms(x, g_ref[...]).astype(BF16)
    z = _dot(h, win_ref[...])
    uv = 0.5 * z * (1.0 + lax.erf(z * math.sqrt(0.5)))
    u = uv[:, :W]
    v = uv[:, W:]
    mu = jnp.mean(v, axis=-1, keepdims=True)
    var = jnp.mean(jnp.square(v - mu), axis=-1, keepdims=True)
    v = (v - mu) * lax.rsqrt(var + EPS) * lng_ref[...] + lnb_ref[...]
    v_ref[...] = v
    vb = v.astype(BF16)
    ri = lax.broadcasted_iota(I32, (GM_CHUNK, GM_CHUNK), 0)
    ci = lax.broadcasted_iota(I32, (GM_CHUNK, GM_CHUNK), 1)
    causal = ci <= ri
    ws = [jnp.where(causal, ws_ref[g], 0.0).astype(BF16) for g in range(GM_GROUPS)]
    chunks = []
    for c in range(tm // GM_CHUNK):
        rs = slice(c * GM_CHUNK, (c + 1) * GM_CHUNK)
        mixed = jnp.concatenate(
            [_dot(ws[g], vb[rs, g * gd:(g + 1) * gd]) for g in range(GM_GROUPS)], axis=1)
        chunks.append(u[rs, :] * (mixed + bs_ref[...]))
    gated = jnp.concatenate(chunks, axis=0).astype(BF16)
    y_ref[...] = x + _dot(gated, wout_ref[...])


def _gmlp(x, g, w_in, ln_g, ln_b, w_s, b_s, w_out, tm, n_prompt_tiles, n_new):
    T, D = x.shape
    W = ln_g.shape[0]
    gd = W // GM_GROUPS
    eye = jnp.eye(GM_CHUNK // n_new, dtype=w_s.dtype)
    ws_sample = jax.vmap(lambda m: jnp.kron(eye, m))(w_s[:, :n_new, :n_new])
    ws_all = jnp.stack([w_s, ws_sample])
    bs_prompt = jnp.repeat(b_s.T, gd, axis=1)
    bs_sample = jnp.tile(bs_prompt[:n_new], (GM_CHUNK // n_new, 1))
    bs_all = jnp.stack([bs_prompt, bs_sample])
    kind = lambda i: jnp.where(i >= n_prompt_tiles, 1, 0)
    row = lambda i: (i, 0)
    const = lambda i: (0, 0)
    n_tiles = T // tm
    return pl.pallas_call(
        _gmlp_kernel,
        grid=(n_tiles,),
        in_specs=[pl.BlockSpec((tm, D), row), pl.BlockSpec((1, D), const),
                  pl.BlockSpec((D, 2 * W), const), pl.BlockSpec((1, W), const), pl.BlockSpec((1, W), const),
                  pl.BlockSpec((None, GM_GROUPS, GM_CHUNK, GM_CHUNK), lambda i: (kind(i), 0, 0, 0)),
                  pl.BlockSpec((None, GM_CHUNK, W), lambda i: (kind(i), 0, 0)),
                  pl.BlockSpec((W, D), const)],
        out_specs=[pl.BlockSpec((tm, D), row),
                   pl.BlockSpec((tm, W), lambda i: (jnp.maximum(i - n_prompt_tiles, 0), 0))],
        out_shape=[jax.ShapeDtypeStruct((T, D), F32),
                   jax.ShapeDtypeStruct(((n_tiles - n_prompt_tiles) * tm, W), F32)],
        compiler_params=_cparams("arbitrary"),
        name="gmlp_mixer",
    )(x, g.reshape(1, D), w_in.astype(BF16), ln_g.reshape(1, W), ln_b.reshape(1, W),
      ws_all, bs_all, w_out.astype(BF16))


def _router_kernel(x_ref, g_ref, wr_ref, h_ref, comb_ref, rank_ref, rankt_ref):
    tm = x_ref.shape[0]
    hf = _rms(x_ref[...], g_ref[...])
    h_ref[...] = hf.astype(BF16)
    logits = jnp.dot(hf, wr_ref[...], preferred_element_type=F32, precision=lax.Precision.HIGHEST)
    lane = lax.broadcasted_iota(I32, logits.shape, 1)
    logits = jnp.where(lane < N_EXPERTS, logits, -jnp.inf)
    m1 = jnp.max(logits, axis=-1, keepdims=True)
    i1 = jnp.min(jnp.where(logits == m1, lane, LANES), axis=-1, keepdims=True)
    rest = jnp.where(lane == i1, -jnp.inf, logits)
    m2 = jnp.max(rest, axis=-1, keepdims=True)
    i2 = jnp.min(jnp.where(rest == m2, lane, LANES), axis=-1, keepdims=True)
    e2 = jnp.exp(m2 - m1)
    den = 1.0 + e2
    comb_ref[...] = jnp.where(lane == i1, 1.0 / den, 0.0) + jnp.where(lane == i2, e2 / den, 0.0)
    routed = jnp.where((lane == i1) | (lane == i2), 1.0, 0.0).astype(BF16)
    r = lax.broadcasted_iota(I32, (tm, tm), 0)
    c = lax.broadcasted_iota(I32, (tm, tm), 1)
    rank = _dot(jnp.where(c < r, 1.0, 0.0).astype(BF16), routed)
    rank = jnp.where(routed > 0, rank, -1.0)
    rank_ref[...] = rank
    rankt_ref[...] = rank.T


def _router(x, g, w_router, tm):
    T, D = x.shape
    wr = jnp.pad(w_router, ((0, 0), (0, LANES - N_EXPERTS)))
    row = lambda i: (i, 0)
    colb = lambda i: (0, i)
    const = lambda i: (0, 0)
    return pl.pallas_call(
        _router_kernel,
        grid=(T // tm,),
        in_specs=[pl.BlockSpec((tm, D), row), pl.BlockSpec((1, D), const), pl.BlockSpec((D, LANES), const)],
        out_specs=[pl.BlockSpec((tm, D), row), pl.BlockSpec((tm, LANES), row),
                   pl.BlockSpec((tm, LANES), row), pl.BlockSpec((LANES, tm), colb)],
        out_shape=[jax.ShapeDtypeStruct((T, D), BF16), jax.ShapeDtypeStruct((T, LANES), F32),
                   jax.ShapeDtypeStruct((T, LANES), F32), jax.ShapeDtypeStruct((LANES, T), F32)],
        compiler_params=_cparams("arbitrary"),
        name="moe_router",
    )(x, g.reshape(1, D), wr)


def _moe_kernel(cnt_ref, h_ref, comb_ref, rank_ref, rankt_ref, wg_ref, wu_ref, wd_ref, acc_ref, *outs,
                e, n_prompt_tiles):
    i = pl.program_id(0)
    tm = h_ref.shape[0]
    gate = comb_ref[:, e:e + 1]
    rank_c = rank_ref[:, e:e + 1]
    rank_r = rankt_ref[e:e + 1, :]
    n_blocks = (cnt_ref[i] + MOE_ROWS - 1) // MOE_ROWS

    y_ref = outs[0] if len(outs) == 1 else outs[2]

    def block(jb, carry):
        base = (jb * MOE_ROWS).astype(F32)
        slot_r = lax.broadcasted_iota(I32, (MOE_ROWS, tm), 0).astype(F32) + base
        take = jnp.where(rank_r == slot_r, 1.0, 0.0).astype(BF16)
        xs = _dot(take, h_ref[...]).astype(BF16)
        a = jax.nn.silu(_dot(xs, wg_ref[...])) * _dot(xs, wu_ref[...])
        y = _dot(a.astype(BF16), wd_ref[...])
        slot_c = lax.broadcasted_iota(I32, (tm, MOE_ROWS), 1).astype(F32) + base
        put = jnp.where(rank_c == slot_c, 1.0, 0.0).astype(BF16)
        y_ref[...] += gate * _dot(put, y.astype(BF16))
        return carry

    y_ref[...] = acc_ref[...]
    lax.fori_loop(0, n_blocks, block, 0)
    if len(outs) > 1:
        @pl.when(i < n_prompt_tiles)
        def _():
            outs[0][...] = y_ref[...]

        @pl.when(i >= n_prompt_tiles)
        def _():
            outs[1][...] = y_ref[...]


def _moe_expert(e, acc, h, comb, rank, rankt, counts_e, w_gate, w_up, w_down, tm, n_prompt_tiles, last):
    T, D = acc.shape
    F = w_gate.shape[2]
    tok = lambda i, cnt: (i, 0)
    once = pl.Buffered(1)
    weight = lambda shape: pl.BlockSpec((None,) + shape, lambda i, cnt: (e, 0, 0), pipeline_mode=once)
    if last:
        npt = n_prompt_tiles
        out_specs = [pl.BlockSpec((tm, D), lambda i, cnt: (jnp.minimum(i, npt - 1), 0)),
                     pl.BlockSpec((tm, D), lambda i, cnt: (jnp.maximum(i - npt, 0), 0))]
        out_shape = [jax.ShapeDtypeStruct((npt * tm, D), F32), jax.ShapeDtypeStruct((T - npt * tm, D), F32)]
        aliases = {}
    else:
        out_specs = pl.BlockSpec((tm, D), tok)
        out_shape = jax.ShapeDtypeStruct((T, D), F32)
        aliases = {8: 0}
    grid_spec = pltpu.PrefetchScalarGridSpec(
        num_scalar_prefetch=1,
        grid=(T // tm,),
        in_specs=[pl.BlockSpec((tm, D), tok), pl.BlockSpec((tm, LANES), tok), pl.BlockSpec((tm, LANES), tok),
                  pl.BlockSpec((LANES, tm), lambda i, cnt: (0, i)),
                  weight((D, F)), weight((D, F)), weight((F, D)),
                  pl.BlockSpec((tm, D), tok)],
        out_specs=out_specs,
        scratch_shapes=[pltpu.VMEM((tm, D), F32)] if last else [],
    )
    return pl.pallas_call(
        functools.partial(_moe_kernel, e=e, n_prompt_tiles=n_prompt_tiles),
        grid_spec=grid_spec,
        out_shape=out_shape,
        input_output_aliases=aliases,
        compiler_params=_cparams("arbitrary"),
        name=f"moe_expert_{e}",
    )(counts_e, h, comb, rank, rankt, w_gate, w_up, w_down, acc)


def _token_tile(n_prompt, n_sample):
    for tm in (512, 256, 128):
        if n_prompt % tm == 0 and n_sample % tm == 0:
            return tm
    raise ValueError("token counts must be multiples of 128")


def _largest_divisor(n, candidates):
    for c in candidates:
        if n % c == 0:
            return c
    return n


def kernel(x_prompt, x_sample, cache_k, cache_v, cache_kidx, page_table, norm_mix, norm_ffn, attn_w_in, attn_g_q, attn_g_k, attn_g_kidx, attn_w_out, gm_w_in, gm_ln_g, gm_ln_b, gm_w_s, gm_b_s, gm_w_out, ffn_w_gate, ffn_w_up, ffn_w_down, moe_w_router, moe_w_gate, moe_w_up, moe_w_down):
    B, S, D = x_prompt.shape
    Bd, n_new, _ = x_sample.shape
    n_pages = page_table.shape[1]
    past = n_pages * PAGE_SIZE
    assert norm_mix.shape[0] == 2 and D == N_HEADS * HEAD_DIM
    n_p, n_s = B * S, Bd * n_new
    T = n_p + n_s
    tm = _token_tile(n_p, n_s)
    npt = n_p // tm
    topk_prompt = min(IDX_TOPK, S // 4)
    topk_sample = min(IDX_TOPK, (past + n_new) // 4)
    d = N_HEADS * HEAD_DIM
    x_p = x_prompt.reshape(n_p, D)
    x_s = x_sample.reshape(n_s, D)

    q, kb, vb, qi, kw, k_p, v_p, k_s, v_s = _attn_project(
        x_p, x_s, norm_mix[0], attn_w_in[0], attn_g_q[0], attn_g_k[0], attn_g_kidx[0], tm)
    tq = _largest_divisor(S, (256, 128, 64))
    o_p = _prompt_attention(q, qi, kw, kb, vb, B, S, tq, topk_prompt)

    C = past + LANES
    qi_s = qi[n_p:].reshape(Bd, n_new * IDX_HEADS, IDX_DIM)
    wi_s = kw[n_p:, IDX_DIM:IDX_DIM + IDX_HEADS].reshape(Bd, n_new * IDX_HEADS, 1)
    ki_new_t = jnp.pad(jnp.swapaxes(kw[n_p:, :IDX_DIM].reshape(Bd, n_new, IDX_DIM), 1, 2),
                       ((0, 0), (0, 0), (0, LANES - n_new)))
    scores = _sample_scores(page_table, qi_s, wi_s, ki_new_t, jnp.swapaxes(cache_kidx[0], 1, 2), C)
    selx, seln = _sample_topk(scores.reshape(n_s, C), topk_sample, past, n_new,
                              _largest_divisor(n_s, (128, 64, 32, 16)))
    flat = PAGE_SIZE * N_HEADS
    n_pool = cache_k.shape[1]
    pg = _largest_divisor(n_pages, tuple(range(MAX_PAGES_PER_STEP, 0, -1)))
    o_s = _sample_attention(
        page_table, q[n_p:].reshape(Bd, n_new * N_HEADS, HEAD_DIM),
        selx.reshape(Bd, n_new, past * N_HEADS), seln.reshape(Bd, n_new, LANES),
        k_s.reshape(Bd, n_new * N_HEADS, HEAD_DIM), v_s.reshape(Bd, n_new * N_HEADS, HEAD_DIM),
        cache_k[0].reshape(n_pool, flat, HEAD_DIM), cache_v[0].reshape(n_pool, flat, HEAD_DIM), pg)
    x = _out_proj(o_p, o_s.reshape(n_s, d), attn_w_out[0], x_p, x_s, tm)
    x = _ffn(x, norm_ffn[0], ffn_w_gate[0], ffn_w_up[0], ffn_w_down[0], tm,
             _largest_divisor(ffn_w_gate.shape[2], (1408, 1024, 512, 256, 128)))

    x, gm_v = _gmlp(x, norm_mix[1], gm_w_in[0], gm_ln_g[0], gm_ln_b[0], gm_w_s[0], gm_b_s[0], gm_w_out[0],
                    tm, npt, n_new)
    h, comb, rank, rankt = _router(x, norm_ffn[1], moe_w_router[0], tm)
    counts = jnp.sum((rankt[:N_EXPERTS] >= 0.0).reshape(N_EXPERTS, T // tm, tm), axis=-1).astype(I32)
    wg, wu, wd = moe_w_gate[0].astype(BF16), moe_w_up[0].astype(BF16), moe_w_down[0].astype(BF16)
    for e in range(N_EXPERTS):
        x = _moe_expert(e, x, h, comb, rank, rankt, counts[e], wg, wu, wd, tm, npt, e == N_EXPERTS - 1)
    y_p, y_s = x

    n_pp = n_p // PAGE_SIZE
    return (y_p.reshape(B, S, D), y_s.reshape(Bd, n_new, D),
            k_p.reshape(1, n_pp, PAGE_SIZE, N_HEADS, HEAD_DIM),
            v_p.reshape(1, n_pp, PAGE_SIZE, N_HEADS, HEAD_DIM),
            kw[:n_p, :IDX_DIM].reshape(1, n_pp, PAGE_SIZE, IDX_DIM),
            k_s.reshape(1, Bd, n_new, N_HEADS, HEAD_DIM),
            v_s.reshape(1, Bd, n_new, N_HEADS, HEAD_DIM),
            kw[n_p:, :IDX_DIM].reshape(1, Bd, n_new, IDX_DIM),
            gm_v.reshape(1, Bd, n_new, gm_ln_g.shape[1]))
```

```python
import functools
import math

import jax
import jax.numpy as jnp
from jax import lax
from jax.experimental import pallas as pl
from jax.experimental.pallas import tpu as pltpu

F32 = jnp.float32
BF16 = jnp.bfloat16
I32 = jnp.int32

N_HEADS = 8
HEAD_DIM = 128
IDX_HEADS = 8
IDX_DIM = 64
IDX_TOPK = 256
PAGE_SIZE = 128
GM_CHUNK = 128
GM_GROUPS = 8
N_EXPERTS = 8
EPS = 1e-6
NEG = -1e30

LANES = 128
MOE_BLOCK_ROWS = (128, 144, 160, 176, 192)
MAX_PAGES_PER_STEP = 16
VMEM_LIMIT = 56 * 1024 * 1024


def _cparams(*sem):
    return pltpu.CompilerParams(dimension_semantics=sem, vmem_limit_bytes=VMEM_LIMIT)


def _dot(a, b):
    return jnp.dot(a, b, preferred_element_type=F32)


def _dot_nt(a, b):
    return lax.dot_general(a, b, (((1,), (1,)), ((), ())), preferred_element_type=F32)


def _rms(x, g):
    return x * lax.rsqrt(jnp.mean(x * x, axis=-1, keepdims=True) + EPS) * g


def _count(m):
    return jnp.sum(jnp.where(m, 1.0, 0.0), axis=1, keepdims=True)


def _topk_select(x_ref, sel_ref, valid_chunk, k):
    R, C = x_ref.shape
    kf = jnp.float32(k)
    x = x_ref[...]
    lo = jnp.min(x, axis=1, keepdims=True)
    hi = jnp.max(x, axis=1, keepdims=True)
    cnt = jnp.full((R, 1), float(C), F32)

    def probe(state, pivot):
        lo, hi, cnt = state
        c = _count(x_ref[...] >= pivot)
        up = (c >= kf) & (pivot > lo)
        dn = (c < kf) & (pivot < hi)
        return jnp.where(up, pivot, lo), jnp.where(dn, pivot, hi), jnp.where(up, c, cnt)

    state = probe((lo, hi, cnt), hi)
    state = probe(state, jnp.full((R, 1), NEG * (1.0 - 2.0 ** -20), F32))
    state = probe(state, jnp.min(jnp.where(x > NEG, x, jnp.inf), axis=1, keepdims=True))
    state = probe(state, jnp.zeros((R, 1), F32))
    state = probe(state, jnp.full((R, 1), jnp.finfo(F32).tiny, F32))
    tied = _count(x > state[0]) < kf

    def midpoint(lo, hi):
        return 0.5 * lo + 0.5 * hi

    def n_open(state):
        lo, hi, cnt = state
        mid = midpoint(lo, hi)
        return jnp.max(jnp.where((cnt > kf) & jnp.logical_not(tied) & (mid > lo) & (mid < hi), 1.0, 0.0))

    def cond(carry):
        it, open_rows, _ = carry
        return (open_rows > 0.0) & (it < 512)

    def body(carry):
        it, _, state = carry
        state = probe(state, midpoint(state[0], state[1]))
        return it + 1, n_open(state), state

    _, _, (tau, _, _) = lax.while_loop(cond, body, (jnp.int32(0), n_open(state), state))
    need = kf - _count(x > tau)
    ri = lax.broadcasted_iota(I32, (LANES, LANES), 0)
    ci = lax.broadcasted_iota(I32, (LANES, LANES), 1)
    upto = jnp.where(ri <= ci, 1.0, 0.0).astype(BF16)
    before = jnp.zeros((R, 1), F32)
    for c0 in range(0, C, LANES):
        xc = x_ref[:, c0:c0 + LANES]
        eq = xc == tau
        run = _dot(jnp.where(eq, 1.0, 0.0).astype(BF16), upto)
        take = (xc > tau) | (eq & (before + run <= need))
        sel_ref[:, c0:c0 + LANES] = jnp.where(take & valid_chunk(c0), 1.0, 0.0).astype(sel_ref.dtype)
        before = before + run[:, LANES - 1:LANES]


def _attn_proj_kernel(xp_ref, xs_ref, g_ref, w_ref, gq_ref, gk_ref, gki_ref,
                      q_ref, kb_ref, vb_ref, qi_ref, kw_ref, kp_ref, vp_ref, ks_ref, vs_ref, *, n_prompt_tiles):
    d = N_HEADS * HEAD_DIM
    tm = xp_ref.shape[0]
    is_prompt = pl.program_id(0) < n_prompt_tiles
    x = jnp.where(is_prompt, xp_ref[...], xs_ref[...])
    h = _rms(x, g_ref[...]).astype(BF16)
    scale = HEAD_DIM ** -0.5

    def head_norm(y, g):
        return y * lax.rsqrt(jnp.mean(y * y, axis=-1, keepdims=True) + EPS) * g

    def put_heads(y, p_ref, s_ref):
        @pl.when(is_prompt)
        def _():
            for hd in range(N_HEADS):
                p_ref[pl.ds(hd, tm, stride=N_HEADS), :] = y[hd]

        @pl.when(jnp.logical_not(is_prompt))
        def _():
            for hd in range(N_HEADS):
                s_ref[pl.ds(hd, tm, stride=N_HEADS), :] = y[hd]

    yq = _dot(h, w_ref[:, 0:d])
    for hd in range(N_HEADS):
        sl = slice(hd * HEAD_DIM, (hd + 1) * HEAD_DIM)
        q_ref[:, sl] = (head_norm(yq[:, sl], gq_ref[...]) * scale).astype(BF16)
    yk = _dot(h, w_ref[:, d:2 * d])
    kn = [head_norm(yk[:, hd * HEAD_DIM:(hd + 1) * HEAD_DIM], gk_ref[...]) for hd in range(N_HEADS)]
    put_heads(kn, kp_ref, ks_ref)
    yv = _dot(h, w_ref[:, 2 * d:3 * d])

    @pl.when(is_prompt)
    def _():
        for hd in range(N_HEADS):
            kb_ref[:, hd * HEAD_DIM:(hd + 1) * HEAD_DIM] = kn[hd].astype(BF16)
        vb_ref[...] = yv.astype(BF16)

    put_heads([yv[:, hd * HEAD_DIM:(hd + 1) * HEAD_DIM] for hd in range(N_HEADS)], vp_ref, vs_ref)
    ni = IDX_HEADS * IDX_DIM
    qi_ref[...] = (_dot(h, w_ref[:, 3 * d:3 * d + ni]) * (IDX_DIM ** -0.5)).astype(BF16)
    yt = _dot(h, w_ref[:, 3 * d + ni:3 * d + ni + LANES])
    lane = lax.broadcasted_iota(I32, yt.shape, 1)
    is_ki = lane < IDX_DIM
    ms = jnp.sum(jnp.where(is_ki, yt * yt, 0.0), axis=-1, keepdims=True) * (1.0 / IDX_DIM)
    kw_ref[...] = jnp.where(is_ki, yt * lax.rsqrt(ms + EPS) * gki_ref[...], yt * (IDX_HEADS ** -0.5))


def _attn_project(x_p, x_s, g_mix, w_in, g_q, g_k, g_kidx, tm):
    n_p, D = x_p.shape
    n_s = x_s.shape[0]
    T = n_p + n_s
    npt = n_p // tm
    d = N_HEADS * HEAD_DIM
    n_in = w_in.shape[1]
    n_pad = 3 * d + IDX_HEADS * IDX_DIM + LANES
    w = jnp.pad(w_in, ((0, 0), (0, n_pad - n_in))).astype(BF16)
    gki = jnp.pad(g_kidx, (0, LANES - IDX_DIM)).reshape(1, LANES)
    row = lambda i: (i, 0)
    prow = lambda i: (jnp.minimum(i, npt - 1), 0)
    srow = lambda i: (jnp.maximum(i - npt, 0), 0)
    const = lambda i: (0, 0)
    stream = lambda n, dt: jax.ShapeDtypeStruct((T, n), dt)
    return pl.pallas_call(
        functools.partial(_attn_proj_kernel, n_prompt_tiles=npt),
        grid=(T // tm,),
        in_specs=[pl.BlockSpec((tm, D), prow), pl.BlockSpec((tm, D), srow), pl.BlockSpec((1, D), const),
                  pl.BlockSpec((D, n_pad), const), pl.BlockSpec((1, HEAD_DIM), const),
                  pl.BlockSpec((1, HEAD_DIM), const), pl.BlockSpec((1, LANES), const)],
        out_specs=[pl.BlockSpec((tm, d), row), pl.BlockSpec((tm, d), prow), pl.BlockSpec((tm, d), prow),
                   pl.BlockSpec((tm, IDX_HEADS * IDX_DIM), row), pl.BlockSpec((tm, LANES), row),
                   pl.BlockSpec((tm * N_HEADS, HEAD_DIM), prow), pl.BlockSpec((tm * N_HEADS, HEAD_DIM), prow),
                   pl.BlockSpec((tm * N_HEADS, HEAD_DIM), srow), pl.BlockSpec((tm * N_HEADS, HEAD_DIM), srow)],
        out_shape=[stream(d, BF16), jax.ShapeDtypeStruct((n_p, d), BF16), jax.ShapeDtypeStruct((n_p, d), BF16),
                   stream(IDX_HEADS * IDX_DIM, BF16), stream(LANES, F32),
                   jax.ShapeDtypeStruct((n_p * N_HEADS, HEAD_DIM), F32),
                   jax.ShapeDtypeStruct((n_p * N_HEADS, HEAD_DIM), F32),
                   jax.ShapeDtypeStruct((n_s * N_HEADS, HEAD_DIM), F32),
                   jax.ShapeDtypeStruct((n_s * N_HEADS, HEAD_DIM), F32)],
        compiler_params=_cparams("arbitrary"),
        name="attn_project",
    )(x_p, x_s, g_mix.reshape(1, D), w, g_q.reshape(1, HEAD_DIM), g_k.reshape(1, HEAD_DIM), gki)


def _prompt_attn_kernel(*refs, topk, t_first, aliased):
    if aliased:
        refs = refs[1:]
    q_ref, qi_ref, kwq_ref, kwk_ref, kb_ref, vb_ref, o_ref, x_ref, sel_ref = refs
    tq, S = x_ref.shape
    t0 = t_first + pl.program_id(1) * tq
    ki = kwk_ref[:, 0:IDX_DIM].astype(BF16)
    qi = qi_ref[...]
    score = None
    for h in range(IDX_HEADS):
        logit = _dot_nt(qi[:, h * IDX_DIM:(h + 1) * IDX_DIM], ki)
        term = jnp.maximum(logit, 0.0) * kwq_ref[:, IDX_DIM + h:IDX_DIM + h + 1]
        score = term if score is None else score + term
    row = lax.broadcasted_iota(I32, (tq, S), 0) + t0
    col = lax.broadcasted_iota(I32, (tq, S), 1)
    x_ref[...] = jnp.where(col <= row, score, NEG)
    rows_c = lax.broadcasted_iota(I32, (tq, LANES), 0) + t0
    cols_c = lax.broadcasted_iota(I32, (tq, LANES), 1)
    _topk_select(x_ref, sel_ref, lambda c0: cols_c + c0 <= rows_c, topk)
    for h in range(N_HEADS):
        sl = slice(h * HEAD_DIM, (h + 1) * HEAD_DIM)
        s = _dot_nt(q_ref[:, sl], kb_ref[:, sl])
        s = jnp.where(sel_ref[...] != 0.0, s, NEG)
        m = jnp.max(s, axis=-1, keepdims=True)
        p = jnp.exp(s - m)
        l = jnp.sum(p, axis=-1, keepdims=True)
        o = _dot(p.astype(BF16), vb_ref[:, sl]) / l
        o_ref[:, sl] = o.astype(o_ref.dtype)


def _prompt_attention(q, qi, kw, kb, vb, B, S, tq, topk):
    d = N_HEADS * HEAD_DIM
    nq = S // tq
    span = 2 if nq % 2 == 0 else 1
    kw3, kb3, vb3 = (a[:B * S].reshape(B, S, a.shape[1]) for a in (kw, kb, vb))
    o = None
    for j in range(nq // span):
        s_eff = (j + 1) * span * tq
        qrow = lambda b, i, j=j: (b * nq + j * span + i, 0)
        batch = lambda b, i: (b, 0, 0)
        in_specs = [pl.BlockSpec((tq, d), qrow), pl.BlockSpec((tq, IDX_HEADS * IDX_DIM), qrow),
                    pl.BlockSpec((tq, LANES), qrow), pl.BlockSpec((None, s_eff, LANES), batch),
                    pl.BlockSpec((None, s_eff, d), batch), pl.BlockSpec((None, s_eff, d), batch)]
        args = [q, qi, kw, kw3, kb3, vb3]
        aliases = {}
        if o is not None:
            in_specs = [pl.BlockSpec(memory_space=pl.ANY)] + in_specs
            args = [o] + args
            aliases = {0: 0}
        o = pl.pallas_call(
            functools.partial(_prompt_attn_kernel, topk=topk, t_first=j * span * tq, aliased=o is not None),
            grid=(B, span),
            in_specs=in_specs,
            out_specs=pl.BlockSpec((tq, d), qrow),
            out_shape=jax.ShapeDtypeStruct((B * S, d), BF16),
            scratch_shapes=[pltpu.VMEM((tq, s_eff), F32), pltpu.VMEM((tq, s_eff), F32)],
            input_output_aliases=aliases,
            compiler_params=_cparams("arbitrary", "arbitrary"),
            name=f"prompt_attention_{j}",
        )(*args)
    return o


def _sample_scores_kernel(pt_ref, qi_ref, w_ref, kin_ref, *rest, n_pages, n_new):
    page_refs = rest[:n_pages]
    s_ref = rest[n_pages]
    kall_ref = rest[n_pages + 1]
    past = n_pages * PAGE_SIZE
    C = kall_ref.shape[1]
    for p in range(n_pages):
        kall_ref[:, p * PAGE_SIZE:(p + 1) * PAGE_SIZE] = page_refs[p][...].astype(BF16)
    kall_ref[:, past:C] = kin_ref[...].astype(BF16)
    logit = _dot(qi_ref[...], kall_ref[...])
    term = jnp.maximum(logit, 0.0) * w_ref[...]
    col = lax.broadcasted_iota(I32, (1, C), 1)
    for t in range(n_new):
        s_t = jnp.sum(term[t * IDX_HEADS:(t + 1) * IDX_HEADS, :], axis=0, keepdims=True)
        s_ref[t:t + 1, :] = jnp.where(col <= past + t, s_t, NEG)


def _sample_scores(page_table, qi_s, wi_s, ki_new_t, kidx_t, C):
    Bd, n_pages = page_table.shape
    n_new = qi_s.shape[1] // IDX_HEADS
    page_spec = lambda p: pl.BlockSpec((None, IDX_DIM, PAGE_SIZE), lambda b, pt: (pt[b, p], 0, 0))
    per_seq = lambda b, pt: (b, 0, 0)
    grid_spec = pltpu.PrefetchScalarGridSpec(
        num_scalar_prefetch=1,
        grid=(Bd,),
        in_specs=[pl.BlockSpec((None, n_new * IDX_HEADS, IDX_DIM), per_seq),
                  pl.BlockSpec((None, n_new * IDX_HEADS, 1), per_seq),
                  pl.BlockSpec((None, IDX_DIM, C - n_pages * PAGE_SIZE), per_seq)]
                 + [page_spec(p) for p in range(n_pages)],
        out_specs=pl.BlockSpec((None, n_new, C), per_seq),
        scratch_shapes=[pltpu.VMEM((IDX_DIM, C), BF16)],
    )
    return pl.pallas_call(
        functools.partial(_sample_scores_kernel, n_pages=n_pages, n_new=n_new),
        grid_spec=grid_spec,
        out_shape=jax.ShapeDtypeStruct((Bd, n_new, C), F32),
        compiler_params=_cparams("arbitrary"),
        name="sample_scores",
    )(page_table, qi_s, wi_s, ki_new_t, *([kidx_t] * n_pages))


def _sample_topk_kernel(s_ref, selx_ref, seln_ref, x_ref, sel_ref, *, topk, past, n_new):
    R, C = x_ref.shape
    flat = PAGE_SIZE * N_HEADS
    x_ref[...] = s_ref[...]
    last_c = past + lax.broadcasted_iota(I32, (R, LANES), 0) % n_new
    cols_c = lax.broadcasted_iota(I32, (R, LANES), 1)
    _topk_select(x_ref, sel_ref, lambda c0: cols_c + c0 <= last_c, topk)
    sel = sel_ref[...]
    e_r = lax.broadcasted_iota(I32, (PAGE_SIZE, flat), 0)
    e_c = lax.broadcasted_iota(I32, (PAGE_SIZE, flat), 1)
    spread = jnp.where(e_c // N_HEADS == e_r, 1.0, 0.0).astype(BF16)
    for p in range(past // PAGE_SIZE):
        selx_ref[:, p * flat:(p + 1) * flat] = _dot(
            sel[:, p * PAGE_SIZE:(p + 1) * PAGE_SIZE], spread).astype(selx_ref.dtype)
    seln_ref[...] = sel[:, past:C]


def _sample_topk(scores, topk, past, n_new, tr):
    R, C = scores.shape
    wide = past * N_HEADS
    row = lambda i: (i, 0)
    return pl.pallas_call(
        functools.partial(_sample_topk_kernel, topk=topk, past=past, n_new=n_new),
        grid=(R // tr,),
        in_specs=[pl.BlockSpec((tr, C), row)],
        out_specs=[pl.BlockSpec((tr, wide), row), pl.BlockSpec((tr, C - past), row)],
        out_shape=[jax.ShapeDtypeStruct((R, wide), BF16), jax.ShapeDtypeStruct((R, C - past), BF16)],
        scratch_shapes=[pltpu.VMEM((tr, C), F32), pltpu.VMEM((tr, C), BF16)],
        compiler_params=_cparams("arbitrary"),
        name="sample_topk",
    )(scores)


def _sample_attn_kernel(pt_ref, q_ref, selx_ref, seln_ref, kn_ref, vn_ref, *rest, pg, n_new):
    k_refs = rest[:pg]
    v_refs = rest[pg:2 * pg]
    o_ref, m_ref, l_ref, acc_ref = rest[2 * pg:]
    g = pl.program_id(1)
    rows = n_new * N_HEADS
    flat = PAGE_SIZE * N_HEADS

    @pl.when(g == 0)
    def _():
        m_ref[...] = jnp.full(m_ref.shape, NEG, F32)
        l_ref[...] = jnp.zeros(l_ref.shape, F32)
        acc_ref[...] = jnp.zeros(acc_ref.shape, F32)

    q = q_ref[...]

    def expand_rows(x):
        return jnp.concatenate(
            [jnp.broadcast_to(x[t:t + 1, :], (N_HEADS, x.shape[1])) for t in range(n_new)], axis=0)

    def update(scores, keeps, values):
        scores = [jnp.where(kp, s, NEG) for s, kp in zip(scores, keeps)]
        m_old = m_ref[...]
        m_new = m_old
        for s in scores:
            m_new = jnp.maximum(m_new, jnp.max(s, axis=-1, keepdims=True))
        alpha = jnp.exp(m_old - m_new)
        l_new = alpha * l_ref[...]
        acc = alpha * acc_ref[...]
        for s, kp, v in zip(scores, keeps, values):
            p = jnp.where(kp, jnp.exp(s - m_new), 0.0)
            l_new = l_new + jnp.sum(p, axis=-1, keepdims=True)
            acc = acc + _dot(p.astype(BF16), v)
        m_ref[...] = m_new
        l_ref[...] = l_new
        acc_ref[...] = acc

    hrow = lax.broadcasted_iota(I32, (rows, flat), 0) % N_HEADS
    hcol = lax.broadcasted_iota(I32, (rows, flat), 1) % N_HEADS
    same_head = hrow == hcol
    scores, keeps, values = [], [], []
    for p in range(pg):
        scores.append(_dot_nt(q, k_refs[p][...].astype(BF16)))
        sel_p = selx_ref[:, p * flat:(p + 1) * flat].astype(F32)
        keeps.append(same_head & (expand_rows(sel_p) > 0.5))
        values.append(v_refs[p][...].astype(BF16))
    update(scores, keeps, values)

    @pl.when(g == pl.num_programs(1) - 1)
    def _():
        s = _dot_nt(q, kn_ref[...].astype(BF16))
        r2 = lax.broadcasted_iota(I32, (rows, rows), 0)
        c2 = lax.broadcasted_iota(I32, (rows, rows), 1)
        sn = seln_ref[...].astype(F32)
        seln = jnp.concatenate(
            [jnp.broadcast_to(sn[:, j:j + 1], (n_new, N_HEADS)) for j in range(n_new)], axis=1)
        keep = (r2 % N_HEADS == c2 % N_HEADS) & (expand_rows(seln) > 0.5)
        update([s], [keep], [vn_ref[...].astype(BF16)])
        o_ref[...] = (acc_ref[...] / l_ref[...]).astype(o_ref.dtype)


def _sample_attention(page_table, q_s, selx, seln, k_new, v_new, cache_k_l, cache_v_l, pg):
    Bd, n_pages = page_table.shape
    rows = q_s.shape[1]
    n_new = rows // N_HEADS
    flat = PAGE_SIZE * N_HEADS
    per_seq = lambda b, g, pt: (b, 0, 0)
    kv_spec = lambda p: pl.BlockSpec((None, flat, HEAD_DIM), lambda b, g, pt: (pt[b, g * pg + p], 0, 0))
    grid_spec = pltpu.PrefetchScalarGridSpec(
        num_scalar_prefetch=1,
        grid=(Bd, n_pages // pg),
        in_specs=[pl.BlockSpec((None, rows, HEAD_DIM), per_seq),
                  pl.BlockSpec((None, n_new, pg * flat), lambda b, g, pt: (b, 0, g)),
                  pl.BlockSpec((None, n_new, seln.shape[2]), per_seq),
                  pl.BlockSpec((None, rows, HEAD_DIM), per_seq),
                  pl.BlockSpec((None, rows, HEAD_DIM), per_seq)]
                 + [kv_spec(p) for p in range(pg)] + [kv_spec(p) for p in range(pg)],
        out_specs=pl.BlockSpec((None, rows, HEAD_DIM), per_seq),
        scratch_shapes=[pltpu.VMEM((rows, 1), F32), pltpu.VMEM((rows, 1), F32),
                        pltpu.VMEM((rows, HEAD_DIM), F32)],
    )
    return pl.pallas_call(
        functools.partial(_sample_attn_kernel, pg=pg, n_new=n_new),
        grid_spec=grid_spec,
        out_shape=jax.ShapeDtypeStruct((Bd, rows, HEAD_DIM), BF16),
        compiler_params=_cparams("arbitrary", "arbitrary"),
        name="sample_attention",
    )(page_table, q_s, selx, seln, k_new, v_new, *([cache_k_l] * pg), *([cache_v_l] * pg))


def _out_proj_kernel(op_ref, os_ref, w_ref, xp_ref, xs_ref, y_ref, *, n_prompt_tiles):
    is_prompt = pl.program_id(0) < n_prompt_tiles
    o = jnp.where(is_prompt, op_ref[...], os_ref[...])
    x = jnp.where(is_prompt, xp_ref[...], xs_ref[...])
    y_ref[...] = x + _dot(o, w_ref[...])


def _out_proj(o_p, o_s, w, x_p, x_s, tm):
    n_p, D = x_p.shape
    T = n_p + x_s.shape[0]
    npt = n_p // tm
    prow = lambda i: (jnp.minimum(i, npt - 1), 0)
    srow = lambda i: (jnp.maximum(i - npt, 0), 0)
    return pl.pallas_call(
        functools.partial(_out_proj_kernel, n_prompt_tiles=npt),
        grid=(T // tm,),
        in_specs=[pl.BlockSpec((tm, o_p.shape[1]), prow), pl.BlockSpec((tm, o_s.shape[1]), srow),
                  pl.BlockSpec(w.shape, lambda i: (0, 0)),
                  pl.BlockSpec((tm, D), prow), pl.BlockSpec((tm, D), srow)],
        out_specs=pl.BlockSpec((tm, D), lambda i: (i, 0)),
        out_shape=jax.ShapeDtypeStruct((T, D), F32),
        compiler_params=_cparams("arbitrary"),
        name="attn_out_proj",
    )(o_p, o_s, w.astype(BF16), x_p, x_s)


def _ffn_kernel(x_ref, g_ref, wg_ref, wu_ref, wd_ref, y_ref, h_ref):
    f = pl.program_id(1)

    @pl.when(f == 0)
    def _():
        x = x_ref[...]
        h_ref[...] = _rms(x, g_ref[...]).astype(BF16)
        y_ref[...] = x

    h = h_ref[...]
    a = jax.nn.silu(_dot(h, wg_ref[...])) * _dot(h, wu_ref[...])
    y_ref[...] += _dot(a.astype(BF16), wd_ref[...])


def _ffn(x, g, w_gate, w_up, w_down, tm, tf):
    T, D = x.shape
    F = w_gate.shape[1]
    row = lambda i, f: (i, 0)
    return pl.pallas_call(
        _ffn_kernel,
        grid=(T // tm, F // tf),
        in_specs=[pl.BlockSpec((tm, D), row), pl.BlockSpec((1, D), lambda i, f: (0, 0)),
                  pl.BlockSpec((D, tf), lambda i, f: (0, f)), pl.BlockSpec((D, tf), lambda i, f: (0, f)),
                  pl.BlockSpec((tf, D), lambda i, f: (f, 0))],
        out_specs=pl.BlockSpec((tm, D), row),
        out_shape=jax.ShapeDtypeStruct((T, D), F32),
        scratch_shapes=[pltpu.VMEM((tm, D), BF16)],
        compiler_params=_cparams("arbitrary", "arbitrary"),
        name="dense_swiglu",
    )(x, g.reshape(1, D), w_gate.astype(BF16), w_up.astype(BF16), w_down.astype(BF16))


def _gmlp_kernel(x_ref, g_ref, win_ref, lng_ref, lnb_ref, ws_ref, bs_ref, wout_ref, y_ref, v_ref):
    tm, D = x_ref.shape
    W = lng_ref.shape[1]
    gd = W // GM_GROUPS
    x = x_ref[...]
    h = _rms(x, g_ref[...]).astype(BF16)
    z = _dot(h, win_ref[...])
    uv = 0.5 * z * (1.0 + lax.erf(z * math.sqrt(0.5)))
    u = uv[:, :W]
    v = uv[:, W:]
    mu = jnp.mean(v, axis=-1, keepdims=True)
    var = jnp.mean(jnp.square(v - mu), axis=-1, keepdims=True)
    v = (v - mu) * lax.rsqrt(var + EPS) * lng_ref[...] + lnb_ref[...]
    v_ref[...] = v
    vb = v.astype(BF16)
    ri = lax.broadcasted_iota(I32, (GM_CHUNK, GM_CHUNK), 0)
    ci = lax.broadcasted_iota(I32, (GM_CHUNK, GM_CHUNK), 1)
    causal = ci <= ri
    ws = [jnp.where(causal, ws_ref[g], 0.0).astype(BF16) for g in range(GM_GROUPS)]
    chunks = []
    for c in range(tm // GM_CHUNK):
        rs = slice(c * GM_CHUNK, (c + 1) * GM_CHUNK)
        mixed = jnp.concatenate(
            [_dot(ws[g], vb[rs, g * gd:(g + 1) * gd]) for g in range(GM_GROUPS)], axis=1)
        chunks.append(u[rs, :] * (mixed + bs_ref[...]))
    gated = jnp.concatenate(chunks, axis=0).astype(BF16)
    y_ref[...] = x + _dot(gated, wout_ref[...])


def _gmlp(x, g, w_in, ln_g, ln_b, w_s, b_s, w_out, tm, n_prompt_tiles, n_new):
    T, D = x.shape
    W = ln_g.shape[0]
    gd = W // GM_GROUPS
    eye = jnp.eye(GM_CHUNK // n_new, dtype=w_s.dtype)
    ws_sample = jax.vmap(lambda m: jnp.kron(eye, m))(w_s[:, :n_new, :n_new])
    ws_all = jnp.stack([w_s, ws_sample])
    bs_prompt = jnp.repeat(b_s.T, gd, axis=1)
    bs_sample = jnp.tile(bs_prompt[:n_new], (GM_CHUNK // n_new, 1))
    bs_all = jnp.stack([bs_prompt, bs_sample])
    kind = lambda i: jnp.where(i >= n_prompt_tiles, 1, 0)
    row = lambda i: (i, 0)
    const = lambda i: (0, 0)
    n_tiles = T // tm
    return pl.pallas_call(
        _gmlp_kernel,
        grid=(n_tiles,),
        in_specs=[pl.BlockSpec((tm, D), row), pl.BlockSpec((1, D), const),
                  pl.BlockSpec((D, 2 * W), const), pl.BlockSpec((1, W), const), pl.BlockSpec((1, W), const),
                  pl.BlockSpec((None, GM_GROUPS, GM_CHUNK, GM_CHUNK), lambda i: (kind(i), 0, 0, 0)),
                  pl.BlockSpec((None, GM_CHUNK, W), lambda i: (kind(i), 0, 0)),
                  pl.BlockSpec((W, D), const)],
        out_specs=[pl.BlockSpec((tm, D), row),
                   pl.BlockSpec((tm, W), lambda i: (jnp.maximum(i - n_prompt_tiles, 0), 0))],
        out_shape=[jax.ShapeDtypeStruct((T, D), F32),
                   jax.ShapeDtypeStruct(((n_tiles - n_prompt_tiles) * tm, W), F32)],
        compiler_params=_cparams("arbitrary"),
        name="gmlp_mixer",
    )(x, g.reshape(1, D), w_in.astype(BF16), ln_g.reshape(1, W), ln_b.reshape(1, W),
      ws_all, bs_all, w_out.astype(BF16))


def _router_kernel(x_ref, g_ref, wr_ref, h_ref, comb_ref, rank_ref, rankt_ref):
    tm = x_ref.shape[0]
    hf = _rms(x_ref[...], g_ref[...])
    h_ref[...] = hf.astype(BF16)
    logits = jnp.dot(hf, wr_ref[...], preferred_element_type=F32, precision=lax.Precision.HIGHEST)
    lane = lax.broadcasted_iota(I32, logits.shape, 1)
    logits = jnp.where(lane < N_EXPERTS, logits, -jnp.inf)
    m1 = jnp.max(logits, axis=-1, keepdims=True)
    i1 = jnp.min(jnp.where(logits == m1, lane, LANES), axis=-1, keepdims=True)
    rest = jnp.where(lane == i1, -jnp.inf, logits)
    m2 = jnp.max(rest, axis=-1, keepdims=True)
    i2 = jnp.min(jnp.where(rest == m2, lane, LANES), axis=-1, keepdims=True)
    e2 = jnp.exp(m2 - m1)
    den = 1.0 + e2
    comb_ref[...] = jnp.where(lane == i1, 1.0 / den, 0.0) + jnp.where(lane == i2, e2 / den, 0.0)
    routed = jnp.where((lane == i1) | (lane == i2), 1.0, 0.0).astype(BF16)
    r = lax.broadcasted_iota(I32, (tm, tm), 0)
    c = lax.broadcasted_iota(I32, (tm, tm), 1)
    rank = _dot(jnp.where(c < r, 1.0, 0.0).astype(BF16), routed)
    rank = jnp.where(routed > 0, rank, -1.0)
    rank_ref[...] = rank
    rankt_ref[...] = rank.T


def _router(x, g, w_router, tm):
    T, D = x.shape
    wr = jnp.pad(w_router, ((0, 0), (0, LANES - N_EXPERTS)))
    row = lambda i: (i, 0)
    colb = lambda i: (0, i)
    const = lambda i: (0, 0)
    return pl.pallas_call(
        _router_kernel,
        grid=(T // tm,),
        in_specs=[pl.BlockSpec((tm, D), row), pl.BlockSpec((1, D), const), pl.BlockSpec((D, LANES), const)],
        out_specs=[pl.BlockSpec((tm, D), row), pl.BlockSpec((tm, LANES), row),
                   pl.BlockSpec((tm, LANES), row), pl.BlockSpec((LANES, tm), colb)],
        out_shape=[jax.ShapeDtypeStruct((T, D), BF16), jax.ShapeDtypeStruct((T, LANES), F32),
                   jax.ShapeDtypeStruct((T, LANES), F32), jax.ShapeDtypeStruct((LANES, T), F32)],
        compiler_params=_cparams("arbitrary"),
        name="moe_router",
    )(x, g.reshape(1, D), wr)


def _moe_kernel(cnt_ref, h_ref, comb_ref, rank_ref, rankt_ref, wg_ref, wu_ref, wd_ref, acc_ref, *outs,
                e, n_prompt_tiles):
    i = pl.program_id(0)
    tm = h_ref.shape[0]
    gate = comb_ref[:, e:e + 1]
    rank_c = rank_ref[:, e:e + 1]
    rank_r = rankt_ref[e:e + 1, :]
    n = cnt_ref[i]
    y_ref = outs[0] if len(outs) == 1 else outs[2]

    def block(first, rows):
        base = first.astype(F32)
        slot_r = lax.broadcasted_iota(I32, (rows, tm), 0).astype(F32) + base
        take = jnp.where(rank_r == slot_r, 1.0, 0.0).astype(BF16)
        xs = _dot(take, h_ref[...]).astype(BF16)
        a = jax.nn.silu(_dot(xs, wg_ref[...])) * _dot(xs, wu_ref[...])
        y = _dot(a.astype(BF16), wd_ref[...])
        slot_c = lax.broadcasted_iota(I32, (tm, rows), 1).astype(F32) + base
        put = jnp.where(rank_c == slot_c, 1.0, 0.0).astype(BF16)
        y_ref[...] += gate * _dot(put, y.astype(BF16))

    y_ref[...] = acc_ref[...]
    smaller = 0
    for rows in MOE_BLOCK_ROWS:
        @pl.when((n > smaller) & (n <= rows))
        def _(rows=rows):
            block(jnp.int32(0), rows)
        smaller = rows

    @pl.when(n > smaller)
    def _():
        def step(jb, carry):
            block(jb * MOE_BLOCK_ROWS[0], MOE_BLOCK_ROWS[0])
            return carry
        lax.fori_loop(0, (n + MOE_BLOCK_ROWS[0] - 1) // MOE_BLOCK_ROWS[0], step, 0)
    if len(outs) > 1:
        @pl.when(i < n_prompt_tiles)
        def _():
            outs[0][...] = y_ref[...]

        @pl.when(i >= n_prompt_tiles)
        def _():
            outs[1][...] = y_ref[...]


def _moe_expert(e, acc, h, comb, rank, rankt, counts_e, w_gate, w_up, w_down, tm, n_prompt_tiles, last):
    T, D = acc.shape
    F = w_gate.shape[2]
    tok = lambda i, cnt: (i, 0)
    once = pl.Buffered(1)
    weight = lambda shape: pl.BlockSpec((None,) + shape, lambda i, cnt: (e, 0, 0), pipeline_mode=once)
    if last:
        npt = n_prompt_tiles
        out_specs = [pl.BlockSpec((tm, D), lambda i, cnt: (jnp.minimum(i, npt - 1), 0)),
                     pl.BlockSpec((tm, D), lambda i, cnt: (jnp.maximum(i - npt, 0), 0))]
        out_shape = [jax.ShapeDtypeStruct((npt * tm, D), F32), jax.ShapeDtypeStruct((T - npt * tm, D), F32)]
        aliases = {}
    else:
        out_specs = pl.BlockSpec((tm, D), tok)
        out_shape = jax.ShapeDtypeStruct((T, D), F32)
        aliases = {8: 0}
    grid_spec = pltpu.PrefetchScalarGridSpec(
        num_scalar_prefetch=1,
        grid=(T // tm,),
        in_specs=[pl.BlockSpec((tm, D), tok), pl.BlockSpec((tm, LANES), tok), pl.BlockSpec((tm, LANES), tok),
                  pl.BlockSpec((LANES, tm), lambda i, cnt: (0, i)),
                  weight((D, F)), weight((D, F)), weight((F, D)),
                  pl.BlockSpec((tm, D), tok)],
        out_specs=out_specs,
        scratch_shapes=[pltpu.VMEM((tm, D), F32)] if last else [],
    )
    return pl.pallas_call(
        functools.partial(_moe_kernel, e=e, n_prompt_tiles=n_prompt_tiles),
        grid_spec=grid_spec,
        out_shape=out_shape,
        input_output_aliases=aliases,
        compiler_params=_cparams("arbitrary"),
        name=f"moe_expert_{e}",
    )(counts_e, h, comb, rank, rankt, w_gate, w_up, w_down, acc)


def _token_tile(n_prompt, n_sample):
    for tm in (512, 256, 128):
        if n_prompt % tm == 0 and n_sample % tm == 0:
            return tm
    raise ValueError("token counts must be multiples of 128")


def _largest_divisor(n, candidates):
    for c in candidates:
        if n % c == 0:
            return c
    return n


def kernel(x_prompt, x_sample, cache_k, cache_v, cache_kidx, page_table, norm_mix, norm_ffn, attn_w_in, attn_g_q, attn_g_k, attn_g_kidx, attn_w_out, gm_w_in, gm_ln_g, gm_ln_b, gm_w_s, gm_b_s, gm_w_out, ffn_w_gate, ffn_w_up, ffn_w_down, moe_w_router, moe_w_gate, moe_w_up, moe_w_down):
    B, S, D = x_prompt.shape
    Bd, n_new, _ = x_sample.shape
    n_pages = page_table.shape[1]
    past = n_pages * PAGE_SIZE
    assert norm_mix.shape[0] == 2 and D == N_HEADS * HEAD_DIM
    n_p, n_s = B * S, Bd * n_new
    T = n_p + n_s
    tm = _token_tile(n_p, n_s)
    npt = n_p // tm
    topk_prompt = min(IDX_TOPK, S // 4)
    topk_sample = min(IDX_TOPK, (past + n_new) // 4)
    d = N_HEADS * HEAD_DIM
    x_p = x_prompt.reshape(n_p, D)
    x_s = x_sample.reshape(n_s, D)

    q, kb, vb, qi, kw, k_p, v_p, k_s, v_s = _attn_project(
        x_p, x_s, norm_mix[0], attn_w_in[0], attn_g_q[0], attn_g_k[0], attn_g_kidx[0], tm)
    tq = _largest_divisor(S, (256, 128, 64))
    o_p = _prompt_attention(q, qi, kw, kb, vb, B, S, tq, topk_prompt)

    C = past + LANES
    qi_s = qi[n_p:].reshape(Bd, n_new * IDX_HEADS, IDX_DIM)
    wi_s = kw[n_p:, IDX_DIM:IDX_DIM + IDX_HEADS].reshape(Bd, n_new * IDX_HEADS, 1)
    ki_new_t = jnp.pad(jnp.swapaxes(kw[n_p:, :IDX_DIM].reshape(Bd, n_new, IDX_DIM), 1, 2),
                       ((0, 0), (0, 0), (0, LANES - n_new)))
    scores = _sample_scores(page_table, qi_s, wi_s, ki_new_t, jnp.swapaxes(cache_kidx[0], 1, 2), C)
    selx, seln = _sample_topk(scores.reshape(n_s, C), topk_sample, past, n_new,
                              _largest_divisor(n_s, (128, 64, 32, 16)))
    flat = PAGE_SIZE * N_HEADS
    n_pool = cache_k.shape[1]
    pg = _largest_divisor(n_pages, tuple(range(MAX_PAGES_PER_STEP, 0, -1)))
    o_s = _sample_attention(
        page_table, q[n_p:].reshape(Bd, n_new * N_HEADS, HEAD_DIM),
        selx.reshape(Bd, n_new, past * N_HEADS), seln.reshape(Bd, n_new, LANES),
        k_s.reshape(Bd, n_new * N_HEADS, HEAD_DIM), v_s.reshape(Bd, n_new * N_HEADS, HEAD_DIM),
        cache_k[0].reshape(n_pool, flat, HEAD_DIM), cache_v[0].reshape(n_pool, flat, HEAD_DIM), pg)
    x = _out_proj(o_p, o_s.reshape(n_s, d), attn_w_out[0], x_p, x_s, tm)
    x = _ffn(x, norm_ffn[0], ffn_w_gate[0], ffn_w_up[0], ffn_w_down[0], tm,
             _largest_divisor(ffn_w_gate.shape[2], (1408, 1024, 512, 256, 128)))

    x, gm_v = _gmlp(x, norm_mix[1], gm_w_in[0], gm_ln_g[0], gm_ln_b[0], gm_w_s[0], gm_b_s[0], gm_w_out[0],
                    tm, npt, n_new)
    h, comb, rank, rankt = _router(x, norm_ffn[1], moe_w_router[0], tm)
    counts = jnp.sum((rankt[:N_EXPERTS] >= 0.0).reshape(N_EXPERTS, T // tm, tm), axis=-1).astype(I32)
    wg, wu, wd = moe_w_gate[0].astype(BF16), moe_w_up[0].astype(BF16), moe_w_down[0].astype(BF16)
    for e in range(N_EXPERTS):
        x = _moe_expert(e, x, h, comb, rank, rankt, counts[e], wg, wu, wd, tm, npt, e == N_EXPERTS - 1)
    y_p, y_s = x

    n_pp = n_p // PAGE_SIZE
    return (y_p.reshape(B, S, D), y_s.reshape(Bd, n_new, D),
            k_p.reshape(1, n_pp, PAGE_SIZE, N_HEADS, HEAD_DIM),
            v_p.reshape(1, n_pp, PAGE_SIZE, N_HEADS, HEAD_DIM),
            kw[:n_p, :IDX_DIM].reshape(1, n_pp, PAGE_SIZE, IDX_DIM),
            k_s.reshape(1, Bd, n_new, N_HEADS, HEAD_DIM),
            v_s.reshape(1, Bd, n_new, N_HEADS, HEAD_DIM),
            kw[n_p:, :IDX_DIM].reshape(1, Bd, n_new, IDX_DIM),
            gm_v.reshape(1, Bd, n_new, gm_ln_g.shape[1]))
```

```python
import functools
import math

import jax
import jax.numpy as jnp
from jax import lax
from jax.experimental import pallas as pl
from jax.experimental.pallas import tpu as pltpu

F32 = jnp.float32
BF16 = jnp.bfloat16
I32 = jnp.int32

N_HEADS = 8
HEAD_DIM = 128
IDX_HEADS = 8
IDX_DIM = 64
IDX_TOPK = 256
PAGE_SIZE = 128
GM_CHUNK = 128
GM_GROUPS = 8
N_EXPERTS = 8
EPS = 1e-6
NEG = -1e30

LANES = 128
MOE_BLOCK_ROWS = (128, 144, 160, 176, 192)
MAX_PAGES_PER_STEP = 16
VMEM_LIMIT = 56 * 1024 * 1024


def _cparams(*sem):
    return pltpu.CompilerParams(dimension_semantics=sem, vmem_limit_bytes=VMEM_LIMIT)


def _dot(a, b):
    return jnp.dot(a, b, preferred_element_type=F32)


def _dot_nt(a, b):
    return lax.dot_general(a, b, (((1,), (1,)), ((), ())), preferred_element_type=F32)


def _rms(x, g):
    return x * lax.rsqrt(jnp.mean(x * x, axis=-1, keepdims=True) + EPS) * g


def _count(m):
    return jnp.sum(jnp.where(m, 1.0, 0.0), axis=1, keepdims=True)


def _topk_select(x_ref, sel_ref, valid_chunk, k):
    R, C = x_ref.shape
    kf = jnp.float32(k)
    x = x_ref[...]
    lo = jnp.min(x, axis=1, keepdims=True)
    hi = jnp.max(x, axis=1, keepdims=True)
    cnt = jnp.full((R, 1), float(C), F32)

    def probe(state, pivot):
        lo, hi, cnt = state
        c = _count(x_ref[...] >= pivot)
        up = (c >= kf) & (pivot > lo)
        dn = (c < kf) & (pivot < hi)
        return jnp.where(up, pivot, lo), jnp.where(dn, pivot, hi), jnp.where(up, c, cnt)

    state = probe((lo, hi, cnt), hi)
    state = probe(state, jnp.full((R, 1), NEG * (1.0 - 2.0 ** -20), F32))
    state = probe(state, jnp.min(jnp.where(x > NEG, x, jnp.inf), axis=1, keepdims=True))
    state = probe(state, jnp.zeros((R, 1), F32))
    state = probe(state, jnp.full((R, 1), jnp.finfo(F32).tiny, F32))
    tied = _count(x > state[0]) < kf

    def midpoint(lo, hi):
        return 0.5 * lo + 0.5 * hi

    def n_open(state):
        lo, hi, cnt = state
        mid = midpoint(lo, hi)
        return jnp.max(jnp.where((cnt > kf) & jnp.logical_not(tied) & (mid > lo) & (mid < hi), 1.0, 0.0))

    def cond(carry):
        it, open_rows, _ = carry
        return (open_rows > 0.0) & (it < 512)

    def body(carry):
        it, _, state = carry
        state = probe(state, midpoint(state[0], state[1]))
        return it + 1, n_open(state), state

    _, _, (tau, _, _) = lax.while_loop(cond, body, (jnp.int32(0), n_open(state), state))
    need = kf - _count(x > tau)
    ri = lax.broadcasted_iota(I32, (LANES, LANES), 0)
    ci = lax.broadcasted_iota(I32, (LANES, LANES), 1)
    upto = jnp.where(ri <= ci, 1.0, 0.0).astype(BF16)
    before = jnp.zeros((R, 1), F32)
    for c0 in range(0, C, LANES):
        xc = x_ref[:, c0:c0 + LANES]
        eq = xc == tau
        run = _dot(jnp.where(eq, 1.0, 0.0).astype(BF16), upto)
        take = (xc > tau) | (eq & (before + run <= need))
        sel_ref[:, c0:c0 + LANES] = jnp.where(take & valid_chunk(c0), 1.0, 0.0).astype(sel_ref.dtype)
        before = before + run[:, LANES - 1:LANES]


def _attn_proj_kernel(xp_ref, xs_ref, g_ref, w_ref, gq_ref, gk_ref, gki_ref,
                      q_ref, kb_ref, vb_ref, qi_ref, kw_ref, kp_ref, vp_ref, ks_ref, vs_ref, *, n_prompt_tiles):
    d = N_HEADS * HEAD_DIM
    is_prompt = pl.program_id(0) < n_prompt_tiles
    x = jnp.where(is_prompt, xp_ref[...], xs_ref[...])
    h = _rms(x, g_ref[...]).astype(BF16)
    scale = HEAD_DIM ** -0.5

    def head_norm(y, g):
        return y * lax.rsqrt(jnp.mean(y * y, axis=-1, keepdims=True) + EPS) * g

    yq = _dot(h, w_ref[:, 0:d])
    for hd in range(N_HEADS):
        sl = slice(hd * HEAD_DIM, (hd + 1) * HEAD_DIM)
        q_ref[:, sl] = (head_norm(yq[:, sl], gq_ref[...]) * scale).astype(BF16)
    yk = _dot(h, w_ref[:, d:2 * d])
    kn = [head_norm(yk[:, hd * HEAD_DIM:(hd + 1) * HEAD_DIM], gk_ref[...]) for hd in range(N_HEADS)]
    yv = _dot(h, w_ref[:, 2 * d:3 * d])

    @pl.when(is_prompt)
    def _():
        for hd in range(N_HEADS):
            kp_ref[:, hd * HEAD_DIM:(hd + 1) * HEAD_DIM] = kn[hd]
            kb_ref[:, hd * HEAD_DIM:(hd + 1) * HEAD_DIM] = kn[hd].astype(BF16)
        vp_ref[...] = yv
        vb_ref[...] = yv.astype(BF16)

    @pl.when(jnp.logical_not(is_prompt))
    def _():
        for hd in range(N_HEADS):
            ks_ref[:, hd * HEAD_DIM:(hd + 1) * HEAD_DIM] = kn[hd]
        vs_ref[...] = yv

    ni = IDX_HEADS * IDX_DIM
    qi_ref[...] = (_dot(h, w_ref[:, 3 * d:3 * d + ni]) * (IDX_DIM ** -0.5)).astype(BF16)
    yt = _dot(h, w_ref[:, 3 * d + ni:3 * d + ni + LANES])
    lane = lax.broadcasted_iota(I32, yt.shape, 1)
    is_ki = lane < IDX_DIM
    ms = jnp.sum(jnp.where(is_ki, yt * yt, 0.0), axis=-1, keepdims=True) * (1.0 / IDX_DIM)
    kw_ref[...] = jnp.where(is_ki, yt * lax.rsqrt(ms + EPS) * gki_ref[...], yt * (IDX_HEADS ** -0.5))


def _attn_project(x_p, x_s, g_mix, w_in, g_q, g_k, g_kidx, tm):
    n_p, D = x_p.shape
    n_s = x_s.shape[0]
    T = n_p + n_s
    npt = n_p // tm
    d = N_HEADS * HEAD_DIM
    n_in = w_in.shape[1]
    n_pad = 3 * d + IDX_HEADS * IDX_DIM + LANES
    w = jnp.pad(w_in, ((0, 0), (0, n_pad - n_in))).astype(BF16)
    gki = jnp.pad(g_kidx, (0, LANES - IDX_DIM)).reshape(1, LANES)
    row = lambda i: (i, 0)
    prow = lambda i: (jnp.minimum(i, npt - 1), 0)
    srow = lambda i: (jnp.maximum(i - npt, 0), 0)
    const = lambda i: (0, 0)
    stream = lambda n, dt: jax.ShapeDtypeStruct((T, n), dt)
    return pl.pallas_call(
        functools.partial(_attn_proj_kernel, n_prompt_tiles=npt),
        grid=(T // tm,),
        in_specs=[pl.BlockSpec((tm, D), prow), pl.BlockSpec((tm, D), srow), pl.BlockSpec((1, D), const),
                  pl.BlockSpec((D, n_pad), const), pl.BlockSpec((1, HEAD_DIM), const),
                  pl.BlockSpec((1, HEAD_DIM), const), pl.BlockSpec((1, LANES), const)],
        out_specs=[pl.BlockSpec((tm, d), row), pl.BlockSpec((tm, d), prow), pl.BlockSpec((tm, d), prow),
                   pl.BlockSpec((tm, IDX_HEADS * IDX_DIM), row), pl.BlockSpec((tm, LANES), row),
                   pl.BlockSpec((tm, d), prow), pl.BlockSpec((tm, d), prow),
                   pl.BlockSpec((tm, d), srow), pl.BlockSpec((tm, d), srow)],
        out_shape=[stream(d, BF16), jax.ShapeDtypeStruct((n_p, d), BF16), jax.ShapeDtypeStruct((n_p, d), BF16),
                   stream(IDX_HEADS * IDX_DIM, BF16), stream(LANES, F32),
                   jax.ShapeDtypeStruct((n_p, d), F32), jax.ShapeDtypeStruct((n_p, d), F32),
                   jax.ShapeDtypeStruct((n_s, d), F32), jax.ShapeDtypeStruct((n_s, d), F32)],
        compiler_params=_cparams("arbitrary"),
        name="attn_project",
    )(x_p, x_s, g_mix.reshape(1, D), w, g_q.reshape(1, HEAD_DIM), g_k.reshape(1, HEAD_DIM), gki)


def _prompt_attn_kernel(*refs, topk, t_first, aliased):
    if aliased:
        refs = refs[1:]
    q_ref, qi_ref, kwq_ref, kwk_ref, kb_ref, vb_ref, o_ref, x_ref, sel_ref = refs
    tq, S = x_ref.shape
    t0 = t_first + pl.program_id(1) * tq
    ki = kwk_ref[:, 0:IDX_DIM].astype(BF16)
    qi = qi_ref[...]
    score = None
    for h in range(IDX_HEADS):
        logit = _dot_nt(qi[:, h * IDX_DIM:(h + 1) * IDX_DIM], ki)
        term = jnp.maximum(logit, 0.0) * kwq_ref[:, IDX_DIM + h:IDX_DIM + h + 1]
        score = term if score is None else score + term
    row = lax.broadcasted_iota(I32, (tq, S), 0) + t0
    col = lax.broadcasted_iota(I32, (tq, S), 1)
    x_ref[...] = jnp.where(col <= row, score, NEG)
    rows_c = lax.broadcasted_iota(I32, (tq, LANES), 0) + t0
    cols_c = lax.broadcasted_iota(I32, (tq, LANES), 1)
    _topk_select(x_ref, sel_ref, lambda c0: cols_c + c0 <= rows_c, topk)
    for h in range(N_HEADS):
        sl = slice(h * HEAD_DIM, (h + 1) * HEAD_DIM)
        s = _dot_nt(q_ref[:, sl], kb_ref[:, sl])
        s = jnp.where(sel_ref[...] != 0.0, s, NEG)
        m = jnp.max(s, axis=-1, keepdims=True)
        p = jnp.exp(s - m)
        l = jnp.sum(p, axis=-1, keepdims=True)
        o = _dot(p.astype(BF16), vb_ref[:, sl]) / l
        o_ref[:, sl] = o.astype(o_ref.dtype)


def _prompt_attention(q, qi, kw, kb, vb, B, S, tq, topk):
    d = N_HEADS * HEAD_DIM
    nq = S // tq
    span = 2 if nq % 2 == 0 else 1
    kw3, kb3, vb3 = (a[:B * S].reshape(B, S, a.shape[1]) for a in (kw, kb, vb))
    o = None
    for j in range(nq // span):
        s_eff = (j + 1) * span * tq
        qrow = lambda b, i, j=j: (b * nq + j * span + i, 0)
        batch = lambda b, i: (b, 0, 0)
        in_specs = [pl.BlockSpec((tq, d), qrow), pl.BlockSpec((tq, IDX_HEADS * IDX_DIM), qrow),
                    pl.BlockSpec((tq, LANES), qrow), pl.BlockSpec((None, s_eff, LANES), batch),
                    pl.BlockSpec((None, s_eff, d), batch), pl.BlockSpec((None, s_eff, d), batch)]
        args = [q, qi, kw, kw3, kb3, vb3]
        aliases = {}
        if o is not None:
            in_specs = [pl.BlockSpec(memory_space=pl.ANY)] + in_specs
            args = [o] + args
            aliases = {0: 0}
        o = pl.pallas_call(
            functools.partial(_prompt_attn_kernel, topk=topk, t_first=j * span * tq, aliased=o is not None),
            grid=(B, span),
            in_specs=in_specs,
            out_specs=pl.BlockSpec((tq, d), qrow),
            out_shape=jax.ShapeDtypeStruct((B * S, d), BF16),
            scratch_shapes=[pltpu.VMEM((tq, s_eff), F32), pltpu.VMEM((tq, s_eff), F32)],
            input_output_aliases=aliases,
            compiler_params=_cparams("arbitrary", "arbitrary"),
            name=f"prompt_attention_{j}",
        )(*args)
    return o


def _sample_scores_kernel(pt_ref, qi_ref, w_ref, kin_ref, *rest, n_pages, n_new):
    page_refs = rest[:n_pages]
    s_ref = rest[n_pages]
    kall_ref = rest[n_pages + 1]
    past = n_pages * PAGE_SIZE
    C = kall_ref.shape[1]
    for p in range(n_pages):
        kall_ref[:, p * PAGE_SIZE:(p + 1) * PAGE_SIZE] = page_refs[p][...].astype(BF16)
    kall_ref[:, past:C] = kin_ref[...].astype(BF16)
    logit = _dot(qi_ref[...], kall_ref[...])
    term = jnp.maximum(logit, 0.0) * w_ref[...]
    col = lax.broadcasted_iota(I32, (1, C), 1)
    for t in range(n_new):
        s_t = jnp.sum(term[t * IDX_HEADS:(t + 1) * IDX_HEADS, :], axis=0, keepdims=True)
        s_ref[t:t + 1, :] = jnp.where(col <= past + t, s_t, NEG)


def _sample_scores(page_table, qi_s, wi_s, ki_new_t, kidx_t, C):
    Bd, n_pages = page_table.shape
    n_new = qi_s.shape[1] // IDX_HEADS
    page_spec = lambda p: pl.BlockSpec((None, IDX_DIM, PAGE_SIZE), lambda b, pt: (pt[b, p], 0, 0))
    per_seq = lambda b, pt: (b, 0, 0)
    grid_spec = pltpu.PrefetchScalarGridSpec(
        num_scalar_prefetch=1,
        grid=(Bd,),
        in_specs=[pl.BlockSpec((None, n_new * IDX_HEADS, IDX_DIM), per_seq),
                  pl.BlockSpec((None, n_new * IDX_HEADS, 1), per_seq),
                  pl.BlockSpec((None, IDX_DIM, C - n_pages * PAGE_SIZE), per_seq)]
                 + [page_spec(p) for p in range(n_pages)],
        out_specs=pl.BlockSpec((None, n_new, C), per_seq),
        scratch_shapes=[pltpu.VMEM((IDX_DIM, C), BF16)],
    )
    return pl.pallas_call(
        functools.partial(_sample_scores_kernel, n_pages=n_pages, n_new=n_new),
        grid_spec=grid_spec,
        out_shape=jax.ShapeDtypeStruct((Bd, n_new, C), F32),
        compiler_params=_cparams("arbitrary"),
        name="sample_scores",
    )(page_table, qi_s, wi_s, ki_new_t, *([kidx_t] * n_pages))


def _sample_topk_kernel(s_ref, selx_ref, seln_ref, x_ref, sel_ref, *, topk, past, n_new):
    R, C = x_ref.shape
    flat = PAGE_SIZE * N_HEADS
    x_ref[...] = s_ref[...]
    last_c = past + lax.broadcasted_iota(I32, (R, LANES), 0) % n_new
    cols_c = lax.broadcasted_iota(I32, (R, LANES), 1)
    _topk_select(x_ref, sel_ref, lambda c0: cols_c + c0 <= last_c, topk)
    sel = sel_ref[...]
    e_r = lax.broadcasted_iota(I32, (PAGE_SIZE, flat), 0)
    e_c = lax.broadcasted_iota(I32, (PAGE_SIZE, flat), 1)
    spread = jnp.where(e_c // N_HEADS == e_r, 1.0, 0.0).astype(BF16)
    for p in range(past // PAGE_SIZE):
        selx_ref[:, p * flat:(p + 1) * flat] = _dot(
            sel[:, p * PAGE_SIZE:(p + 1) * PAGE_SIZE], spread).astype(selx_ref.dtype)
    seln_ref[...] = sel[:, past:C]


def _sample_topk(scores, topk, past, n_new, tr):
    R, C = scores.shape
    wide = past * N_HEADS
    row = lambda i: (i, 0)
    return pl.pallas_call(
        functools.partial(_sample_topk_kernel, topk=topk, past=past, n_new=n_new),
        grid=(R // tr,),
        in_specs=[pl.BlockSpec((tr, C), row)],
        out_specs=[pl.BlockSpec((tr, wide), row), pl.BlockSpec((tr, C - past), row)],
        out_shape=[jax.ShapeDtypeStruct((R, wide), BF16), jax.ShapeDtypeStruct((R, C - past), BF16)],
        scratch_shapes=[pltpu.VMEM((tr, C), F32), pltpu.VMEM((tr, C), BF16)],
        compiler_params=_cparams("arbitrary"),
        name="sample_topk",
    )(scores)


def _sample_attn_kernel(pt_ref, q_ref, selx_ref, seln_ref, kn_ref, vn_ref, *rest, pg, n_new):
    k_refs = rest[:pg]
    v_refs = rest[pg:2 * pg]
    o_ref, m_ref, l_ref, acc_ref = rest[2 * pg:]
    g = pl.program_id(1)
    rows = n_new * N_HEADS
    flat = PAGE_SIZE * N_HEADS

    @pl.when(g == 0)
    def _():
        m_ref[...] = jnp.full(m_ref.shape, NEG, F32)
        l_ref[...] = jnp.zeros(l_ref.shape, F32)
        acc_ref[...] = jnp.zeros(acc_ref.shape, F32)

    q = q_ref[...]

    def expand_rows(x):
        return jnp.concatenate(
            [jnp.broadcast_to(x[t:t + 1, :], (N_HEADS, x.shape[1])) for t in range(n_new)], axis=0)

    def update(scores, keeps, values):
        scores = [jnp.where(kp, s, NEG) for s, kp in zip(scores, keeps)]
        m_old = m_ref[...]
        m_new = m_old
        for s in scores:
            m_new = jnp.maximum(m_new, jnp.max(s, axis=-1, keepdims=True))
        alpha = jnp.exp(m_old - m_new)
        l_new = alpha * l_ref[...]
        acc = alpha * acc_ref[...]
        for s, kp, v in zip(scores, keeps, values):
            p = jnp.where(kp, jnp.exp(s - m_new), 0.0)
            l_new = l_new + jnp.sum(p, axis=-1, keepdims=True)
            acc = acc + _dot(p.astype(BF16), v)
        m_ref[...] = m_new
        l_ref[...] = l_new
        acc_ref[...] = acc

    hrow = lax.broadcasted_iota(I32, (rows, flat), 0) % N_HEADS
    hcol = lax.broadcasted_iota(I32, (rows, flat), 1) % N_HEADS
    same_head = hrow == hcol
    scores, keeps, values = [], [], []
    for p in range(pg):
        scores.append(_dot_nt(q, k_refs[p][...].astype(BF16)))
        sel_p = selx_ref[:, p * flat:(p + 1) * flat].astype(F32)
        keeps.append(same_head & (expand_rows(sel_p) > 0.5))
        values.append(v_refs[p][...].astype(BF16))
    update(scores, keeps, values)

    @pl.when(g == pl.num_programs(1) - 1)
    def _():
        s = _dot_nt(q, kn_ref[...].astype(BF16))
        r2 = lax.broadcasted_iota(I32, (rows, rows), 0)
        c2 = lax.broadcasted_iota(I32, (rows, rows), 1)
        sn = seln_ref[...].astype(F32)
        seln = jnp.concatenate(
            [jnp.broadcast_to(sn[:, j:j + 1], (n_new, N_HEADS)) for j in range(n_new)], axis=1)
        keep = (r2 % N_HEADS == c2 % N_HEADS) & (expand_rows(seln) > 0.5)
        update([s], [keep], [vn_ref[...].astype(BF16)])
        o_ref[...] = (acc_ref[...] / l_ref[...]).astype(o_ref.dtype)


def _sample_attention(page_table, q_s, selx, seln, k_new, v_new, cache_k_l, cache_v_l, pg):
    Bd, n_pages = page_table.shape
    rows = q_s.shape[1]
    n_new = rows // N_HEADS
    flat = PAGE_SIZE * N_HEADS
    per_seq = lambda b, g, pt: (b, 0, 0)
    kv_spec = lambda p: pl.BlockSpec((None, flat, HEAD_DIM), lambda b, g, pt: (pt[b, g * pg + p], 0, 0))
    grid_spec = pltpu.PrefetchScalarGridSpec(
        num_scalar_prefetch=1,
        grid=(Bd, n_pages // pg),
        in_specs=[pl.BlockSpec((None, rows, HEAD_DIM), per_seq),
                  pl.BlockSpec((None, n_new, pg * flat), lambda b, g, pt: (b, 0, g)),
                  pl.BlockSpec((None, n_new, seln.shape[2]), per_seq),
                  pl.BlockSpec((None, rows, HEAD_DIM), per_seq),
                  pl.BlockSpec((None, rows, HEAD_DIM), per_seq)]
                 + [kv_spec(p) for p in range(pg)] + [kv_spec(p) for p in range(pg)],
        out_specs=pl.BlockSpec((None, rows, HEAD_DIM), per_seq),
        scratch_shapes=[pltpu.VMEM((rows, 1), F32), pltpu.VMEM((rows, 1), F32),
                        pltpu.VMEM((rows, HEAD_DIM), F32)],
    )
    return pl.pallas_call(
        functools.partial(_sample_attn_kernel, pg=pg, n_new=n_new),
        grid_spec=grid_spec,
        out_shape=jax.ShapeDtypeStruct((Bd, rows, HEAD_DIM), BF16),
        compiler_params=_cparams("arbitrary", "arbitrary"),
        name="sample_attention",
    )(page_table, q_s, selx, seln, k_new, v_new, *([cache_k_l] * pg), *([cache_v_l] * pg))


def _out_proj_kernel(op_ref, os_ref, w_ref, xp_ref, xs_ref, y_ref, *, n_prompt_tiles):
    is_prompt = pl.program_id(0) < n_prompt_tiles
    o = jnp.where(is_prompt, op_ref[...], os_ref[...])
    x = jnp.where(is_prompt, xp_ref[...], xs_ref[...])
    y_ref[...] = x + _dot(o, w_ref[...])


def _out_proj(o_p, o_s, w, x_p, x_s, tm):
    n_p, D = x_p.shape
    T = n_p + x_s.shape[0]
    npt = n_p // tm
    prow = lambda i: (jnp.minimum(i, npt - 1), 0)
    srow = lambda i: (jnp.maximum(i - npt, 0), 0)
    return pl.pallas_call(
        functools.partial(_out_proj_kernel, n_prompt_tiles=npt),
        grid=(T // tm,),
        in_specs=[pl.BlockSpec((tm, o_p.shape[1]), prow), pl.BlockSpec((tm, o_s.shape[1]), srow),
                  pl.BlockSpec(w.shape, lambda i: (0, 0)),
                  pl.BlockSpec((tm, D), prow), pl.BlockSpec((tm, D), srow)],
        out_specs=pl.BlockSpec((tm, D), lambda i: (i, 0)),
        out_shape=jax.ShapeDtypeStruct((T, D), F32),
        compiler_params=_cparams("arbitrary"),
        name="attn_out_proj",
    )(o_p, o_s, w.astype(BF16), x_p, x_s)


def _ffn_kernel(x_ref, g_ref, wg_ref, wu_ref, wd_ref, y_ref, h_ref):
    f = pl.program_id(1)

    @pl.when(f == 0)
    def _():
        x = x_ref[...]
        h_ref[...] = _rms(x, g_ref[...]).astype(BF16)
        y_ref[...] = x

    h = h_ref[...]
    a = jax.nn.silu(_dot(h, wg_ref[...])) * _dot(h, wu_ref[...])
    y_ref[...] += _dot(a.astype(BF16), wd_ref[...])


def _ffn(x, g, w_gate, w_up, w_down, tm, tf):
    T, D = x.shape
    F = w_gate.shape[1]
    row = lambda i, f: (i, 0)
    return pl.pallas_call(
        _ffn_kernel,
        grid=(T // tm, F // tf),
        in_specs=[pl.BlockSpec((tm, D), row), pl.BlockSpec((1, D), lambda i, f: (0, 0)),
                  pl.BlockSpec((D, tf), lambda i, f: (0, f)), pl.BlockSpec((D, tf), lambda i, f: (0, f)),
                  pl.BlockSpec((tf, D), lambda i, f: (f, 0))],
        out_specs=pl.BlockSpec((tm, D), row),
        out_shape=jax.ShapeDtypeStruct((T, D), F32),
        scratch_shapes=[pltpu.VMEM((tm, D), BF16)],
        compiler_params=_cparams("arbitrary", "arbitrary"),
        name="dense_swiglu",
    )(x, g.reshape(1, D), w_gate.astype(BF16), w_up.astype(BF16), w_down.astype(BF16))


def _gmlp_kernel(x_ref, g_ref, win_ref, lng_ref, lnb_ref, ws_ref, bs_ref, wout_ref, y_ref, v_ref):
    tm, D = x_ref.shape
    W = lng_ref.shape[1]
    gd = W // GM_GROUPS
    x = x_ref[...]
    h = _rms(x, g_ref[...]).astype(BF16)
    z = _dot(h, win_ref[...])
    uv = 0.5 * z * (1.0 + lax.erf(z * math.sqrt(0.5)))
    u = uv[:, :W]
    v = uv[:, W:]
    mu = jnp.mean(v, axis=-1, keepdims=True)
    var = jnp.mean(jnp.square(v - mu), axis=-1, keepdims=True)
    v = (v - mu) * lax.rsqrt(var + EPS) * lng_ref[...] + lnb_ref[...]
    v_ref[...] = v
    vb = v.astype(BF16)
    ri = lax.broadcasted_iota(I32, (GM_CHUNK, GM_CHUNK), 0)
    ci = lax.broadcasted_iota(I32, (GM_CHUNK, GM_CHUNK), 1)
    causal = ci <= ri
    ws = [jnp.where(causal, ws_ref[g], 0.0).astype(BF16) for g in range(GM_GROUPS)]
    chunks = []
    for c in range(tm // GM_CHUNK):
        rs = slice(c * GM_CHUNK, (c + 1) * GM_CHUNK)
        mixed = jnp.concatenate(
            [_dot(ws[g], vb[rs, g * gd:(g + 1) * gd]) for g in range(GM_GROUPS)], axis=1)
        chunks.append(u[rs, :] * (mixed + bs_ref[...]))
    gated = jnp.concatenate(chunks, axis=0).astype(BF16)
    y_ref[...] = x + _dot(gated, wout_ref[...])


def _gmlp(x, g, w_in, ln_g, ln_b, w_s, b_s, w_out, tm, n_prompt_tiles, n_new):
    T, D = x.shape
    W = ln_g.shape[0]
    gd = W // GM_GROUPS
    eye = jnp.eye(GM_CHUNK // n_new, dtype=w_s.dtype)
    ws_sample = jax.vmap(lambda m: jnp.kron(eye, m))(w_s[:, :n_new, :n_new])
    ws_all = jnp.stack([w_s, ws_sample])
    bs_prompt = jnp.repeat(b_s.T, gd, axis=1)
    bs_sample = jnp.tile(bs_prompt[:n_new], (GM_CHUNK // n_new, 1))
    bs_all = jnp.stack([bs_prompt, bs_sample])
    kind = lambda i: jnp.where(i >= n_prompt_tiles, 1, 0)
    row = lambda i: (i, 0)
    const = lambda i: (0, 0)
    n_tiles = T // tm
    return pl.pallas_call(
        _gmlp_kernel,
        grid=(n_tiles,),
        in_specs=[pl.BlockSpec((tm, D), row), pl.BlockSpec((1, D), const),
                  pl.BlockSpec((D, 2 * W), const), pl.BlockSpec((1, W), const), pl.BlockSpec((1, W), const),
                  pl.BlockSpec((None, GM_GROUPS, GM_CHUNK, GM_CHUNK), lambda i: (kind(i), 0, 0, 0)),
                  pl.BlockSpec((None, GM_CHUNK, W), lambda i: (kind(i), 0, 0)),
                  pl.BlockSpec((W, D), const)],
        out_specs=[pl.BlockSpec((tm, D), row),
                   pl.BlockSpec((tm, W), lambda i: (jnp.maximum(i - n_prompt_tiles, 0), 0))],
        out_shape=[jax.ShapeDtypeStruct((T, D), F32),
                   jax.ShapeDtypeStruct(((n_tiles - n_prompt_tiles) * tm, W), F32)],
        compiler_params=_cparams("arbitrary"),
        name="gmlp_mixer",
    )(x, g.reshape(1, D), w_in.astype(BF16), ln_g.reshape(1, W), ln_b.reshape(1, W),
      ws_all, bs_all, w_out.astype(BF16))


def _router_kernel(x_ref, g_ref, wr_ref, h_ref, comb_ref, rank_ref, rankt_ref):
    tm = x_ref.shape[0]
    hf = _rms(x_ref[...], g_ref[...])
    h_ref[...] = hf.astype(BF16)
    logits = jnp.dot(hf, wr_ref[...], preferred_element_type=F32, precision=lax.Precision.HIGHEST)
    lane = lax.broadcasted_iota(I32, logits.shape, 1)
    logits = jnp.where(lane < N_EXPERTS, logits, -jnp.inf)
    m1 = jnp.max(logits, axis=-1, keepdims=True)
    i1 = jnp.min(jnp.where(logits == m1, lane, LANES), axis=-1, keepdims=True)
    rest = jnp.where(lane == i1, -jnp.inf, logits)
    m2 = jnp.max(rest, axis=-1, keepdims=True)
    i2 = jnp.min(jnp.where(rest == m2, lane, LANES), axis=-1, keepdims=True)
    e2 = jnp.exp(m2 - m1)
    den = 1.0 + e2
    comb_ref[...] = jnp.where(lane == i1, 1.0 / den, 0.0) + jnp.where(lane == i2, e2 / den, 0.0)
    routed = jnp.where((lane == i1) | (lane == i2), 1.0, 0.0).astype(BF16)
    r = lax.broadcasted_iota(I32, (tm, tm), 0)
    c = lax.broadcasted_iota(I32, (tm, tm), 1)
    rank = _dot(jnp.where(c < r, 1.0, 0.0).astype(BF16), routed)
    rank = jnp.where(routed > 0, rank, -1.0)
    rank_ref[...] = rank
    rankt_ref[...] = rank.T


def _router(x, g, w_router, tm):
    T, D = x.shape
    wr = jnp.pad(w_router, ((0, 0), (0, LANES - N_EXPERTS)))
    row = lambda i: (i, 0)
    colb = lambda i: (0, i)
    const = lambda i: (0, 0)
    return pl.pallas_call(
        _router_kernel,
        grid=(T // tm,),
        in_specs=[pl.BlockSpec((tm, D), row), pl.BlockSpec((1, D), const), pl.BlockSpec((D, LANES), const)],
        out_specs=[pl.BlockSpec((tm, D), row), pl.BlockSpec((tm, LANES), row),
                   pl.BlockSpec((tm, LANES), row), pl.BlockSpec((LANES, tm), colb)],
        out_shape=[jax.ShapeDtypeStruct((T, D), BF16), jax.ShapeDtypeStruct((T, LANES), F32),
                   jax.ShapeDtypeStruct((T, LANES), F32), jax.ShapeDtypeStruct((LANES, T), F32)],
        compiler_params=_cparams("arbitrary"),
        name="moe_router",
    )(x, g.reshape(1, D), wr)


def _moe_kernel(cnt_ref, h_ref, comb_ref, rank_ref, rankt_ref, wg_ref, wu_ref, wd_ref, acc_ref, *rest,
                e, n_prompt_tiles, n_chunks, n_out):
    outs = rest[:n_out]
    wg_s, wu_s, wd_s = rest[n_out:n_out + 3]
    y_ref = outs[0] if n_out == 1 else rest[n_out + 3]
    s = pl.program_id(0)
    tm = h_ref.shape[0]

    tf = wg_ref.shape[1]
    for c in range(n_chunks):
        @pl.when(s == c)
        def _(c=c):
            wg_s[:, c * tf:(c + 1) * tf] = wg_ref[...].astype(BF16)
            wu_s[:, c * tf:(c + 1) * tf] = wu_ref[...].astype(BF16)
            wd_s[c * tf:(c + 1) * tf, :] = wd_ref[...].astype(BF16)

    @pl.when(s >= n_chunks - 1)
    def _():
        i = s - (n_chunks - 1)
        gate = comb_ref[:, e:e + 1]
        rank_c = rank_ref[:, e:e + 1]
        rank_r = rankt_ref[e:e + 1, :]
        n = cnt_ref[i]

        def block(first, rows):
            base = first.astype(F32)
            slot_r = lax.broadcasted_iota(I32, (rows, tm), 0).astype(F32) + base
            take = jnp.where(rank_r == slot_r, 1.0, 0.0).astype(BF16)
            xs = _dot(take, h_ref[...]).astype(BF16)
            a = jax.nn.silu(_dot(xs, wg_s[...])) * _dot(xs, wu_s[...])
            y = _dot(a.astype(BF16), wd_s[...])
            slot_c = lax.broadcasted_iota(I32, (tm, rows), 1).astype(F32) + base
            put = jnp.where(rank_c == slot_c, 1.0, 0.0).astype(BF16)
            y_ref[...] += gate * _dot(put, y.astype(BF16))

        y_ref[...] = acc_ref[...]
        smaller = 0
        for rows in MOE_BLOCK_ROWS:
            @pl.when((n > smaller) & (n <= rows))
            def _(rows=rows):
                block(jnp.int32(0), rows)
            smaller = rows

        @pl.when(n > smaller)
        def _():
            def step(jb, carry):
                block(jb * MOE_BLOCK_ROWS[0], MOE_BLOCK_ROWS[0])
                return carry
            lax.fori_loop(0, (n + MOE_BLOCK_ROWS[0] - 1) // MOE_BLOCK_ROWS[0], step, 0)

        if n_out > 1:
            @pl.when(i < n_prompt_tiles)
            def _():
                outs[0][...] = y_ref[...]

            @pl.when(i >= n_prompt_tiles)
            def _():
                outs[1][...] = y_ref[...]


def _moe_expert(e, acc, h, comb, rank, rankt, counts_e, w_gate, w_up, w_down, tm, n_prompt_tiles, last):
    T, D = acc.shape
    F = w_gate.shape[2]
    tf = _largest_divisor(F, (512, 256, 128))
    nc = F // tf
    npt = n_prompt_tiles
    tile = lambda s: jnp.maximum(s - (nc - 1), 0)
    chunk = lambda s: jnp.minimum(s, nc - 1)
    tok = lambda s, cnt: (tile(s), 0)
    if last:
        out_specs = [pl.BlockSpec((tm, D), lambda s, cnt: (jnp.minimum(tile(s), npt - 1), 0)),
                     pl.BlockSpec((tm, D), lambda s, cnt: (jnp.maximum(tile(s) - npt, 0), 0))]
        out_shape = [jax.ShapeDtypeStruct((npt * tm, D), F32), jax.ShapeDtypeStruct((T - npt * tm, D), F32)]
        aliases = {}
    else:
        out_specs = pl.BlockSpec((tm, D), tok)
        out_shape = jax.ShapeDtypeStruct((T, D), F32)
        aliases = {8: 0}
    grid_spec = pltpu.PrefetchScalarGridSpec(
        num_scalar_prefetch=1,
        grid=(T // tm + nc - 1,),
        in_specs=[pl.BlockSpec((tm, D), tok), pl.BlockSpec((tm, LANES), tok), pl.BlockSpec((tm, LANES), tok),
                  pl.BlockSpec((LANES, tm), lambda s, cnt: (0, tile(s))),
                  pl.BlockSpec((None, D, tf), lambda s, cnt: (e, 0, chunk(s))),
                  pl.BlockSpec((None, D, tf), lambda s, cnt: (e, 0, chunk(s))),
                  pl.BlockSpec((None, tf, D), lambda s, cnt: (e, chunk(s), 0)),
                  pl.BlockSpec((tm, D), tok)],
        out_specs=out_specs,
        scratch_shapes=[pltpu.VMEM((D, F), BF16), pltpu.VMEM((D, F), BF16),
                        pltpu.VMEM((F, D), BF16)] + ([pltpu.VMEM((tm, D), F32)] if last else []),
    )
    return pl.pallas_call(
        functools.partial(_moe_kernel, e=e, n_prompt_tiles=npt, n_chunks=nc, n_out=2 if last else 1),
        grid_spec=grid_spec,
        out_shape=out_shape,
        input_output_aliases=aliases,
        compiler_params=_cparams("arbitrary"),
        name=f"moe_expert_{e}",
    )(counts_e, h, comb, rank, rankt, w_gate, w_up, w_down, acc)


def _token_tile(n_prompt, n_sample):
    for tm in (512, 256, 128):
        if n_prompt % tm == 0 and n_sample % tm == 0:
            return tm
    raise ValueError("token counts must be multiples of 128")


def _largest_divisor(n, candidates):
    for c in candidates:
        if n % c == 0:
            return c
    return n


def kernel(x_prompt, x_sample, cache_k, cache_v, cache_kidx, page_table, norm_mix, norm_ffn, attn_w_in, attn_g_q, attn_g_k, attn_g_kidx, attn_w_out, gm_w_in, gm_ln_g, gm_ln_b, gm_w_s, gm_b_s, gm_w_out, ffn_w_gate, ffn_w_up, ffn_w_down, moe_w_router, moe_w_gate, moe_w_up, moe_w_down):
    B, S, D = x_prompt.shape
    Bd, n_new, _ = x_sample.shape
    n_pages = page_table.shape[1]
    past = n_pages * PAGE_SIZE
    assert norm_mix.shape[0] == 2 and D == N_HEADS * HEAD_DIM
    n_p, n_s = B * S, Bd * n_new
    T = n_p + n_s
    tm = _token_tile(n_p, n_s)
    npt = n_p // tm
    topk_prompt = min(IDX_TOPK, S // 4)
    topk_sample = min(IDX_TOPK, (past + n_new) // 4)
    d = N_HEADS * HEAD_DIM
    x_p = x_prompt.reshape(n_p, D)
    x_s = x_sample.reshape(n_s, D)

    q, kb, vb, qi, kw, k_p, v_p, k_s, v_s = _attn_project(
        x_p, x_s, norm_mix[0], attn_w_in[0], attn_g_q[0], attn_g_k[0], attn_g_kidx[0], tm)
    tq = _largest_divisor(S, (256, 128, 64))
    o_p = _prompt_attention(q, qi, kw, kb, vb, B, S, tq, topk_prompt)

    C = past + LANES
    qi_s = qi[n_p:].reshape(Bd, n_new * IDX_HEADS, IDX_DIM)
    wi_s = kw[n_p:, IDX_DIM:IDX_DIM + IDX_HEADS].reshape(Bd, n_new * IDX_HEADS, 1)
    ki_new_t = jnp.pad(jnp.swapaxes(kw[n_p:, :IDX_DIM].reshape(Bd, n_new, IDX_DIM), 1, 2),
                       ((0, 0), (0, 0), (0, LANES - n_new)))
    scores = _sample_scores(page_table, qi_s, wi_s, ki_new_t, jnp.swapaxes(cache_kidx[0], 1, 2), C)
    selx, seln = _sample_topk(scores.reshape(n_s, C), topk_sample, past, n_new,
                              _largest_divisor(n_s, (128, 64, 32, 16)))
    flat = PAGE_SIZE * N_HEADS
    n_pool = cache_k.shape[1]
    pg = _largest_divisor(n_pages, tuple(range(MAX_PAGES_PER_STEP, 0, -1)))
    o_s = _sample_attention(
        page_table, q[n_p:].reshape(Bd, n_new * N_HEADS, HEAD_DIM),
        selx.reshape(Bd, n_new, past * N_HEADS), seln.reshape(Bd, n_new, LANES),
        k_s.reshape(Bd, n_new * N_HEADS, HEAD_DIM), v_s.reshape(Bd, n_new * N_HEADS, HEAD_DIM),
        cache_k[0].reshape(n_pool, flat, HEAD_DIM), cache_v[0].reshape(n_pool, flat, HEAD_DIM), pg)
    x = _out_proj(o_p, o_s.reshape(n_s, d), attn_w_out[0], x_p, x_s, tm)
    x = _ffn(x, norm_ffn[0], ffn_w_gate[0], ffn_w_up[0], ffn_w_down[0], tm,
             _largest_divisor(ffn_w_gate.shape[2], (1408, 1024, 512, 256, 128)))

    x, gm_v = _gmlp(x, norm_mix[1], gm_w_in[0], gm_ln_g[0], gm_ln_b[0], gm_w_s[0], gm_b_s[0], gm_w_out[0],
                    tm, npt, n_new)
    h, comb, rank, rankt = _router(x, norm_ffn[1], moe_w_router[0], tm)
    counts = jnp.sum((rankt[:N_EXPERTS] >= 0.0).reshape(N_EXPERTS, T // tm, tm), axis=-1).astype(I32)
    wg, wu, wd = moe_w_gate[0], moe_w_up[0], moe_w_down[0]
    for e in range(N_EXPERTS):
        x = _moe_expert(e, x, h, comb, rank, rankt, counts[e], wg, wu, wd, tm, npt, e == N_EXPERTS - 1)
    y_p, y_s = x

    n_pp = n_p // PAGE_SIZE
    return (y_p.reshape(B, S, D), y_s.reshape(Bd, n_new, D),
            k_p.reshape(1, n_pp, PAGE_SIZE, N_HEADS, HEAD_DIM),
            v_p.reshape(1, n_pp, PAGE_SIZE, N_HEADS, HEAD_DIM),
            kw[:n_p, :IDX_DIM].reshape(1, n_pp, PAGE_SIZE, IDX_DIM),
            k_s.reshape(1, Bd, n_new, N_HEADS, HEAD_DIM),
            v_s.reshape(1, Bd, n_new, N_HEADS, HEAD_DIM),
            kw[n_p:, :IDX_DIM].reshape(1, Bd, n_new, IDX_DIM),
            gm_v.reshape(1, Bd, n_new, gm_ln_g.shape[1]))
```

```python
import functools
import math

import jax
import jax.numpy as jnp
from jax import lax
from jax.experimental import pallas as pl
from jax.experimental.pallas import tpu as pltpu

F32 = jnp.float32
BF16 = jnp.bfloat16
I32 = jnp.int32

N_HEADS = 8
HEAD_DIM = 128
IDX_HEADS = 8
IDX_DIM = 64
IDX_TOPK = 256
PAGE_SIZE = 128
GM_CHUNK = 128
GM_GROUPS = 8
N_EXPERTS = 8
EPS = 1e-6
NEG = -1e30

LANES = 128
MOE_BLOCK_ROWS = (128, 144, 160, 176, 192)
MAX_PAGES_PER_STEP = 16
HALVING_STEPS = 12
SNAP_EXCESS = 2.0
VMEM_LIMIT = 56 * 1024 * 1024


def _cparams(*sem):
    return pltpu.CompilerParams(dimension_semantics=sem, vmem_limit_bytes=VMEM_LIMIT)


def _dot(a, b):
    return jnp.dot(a, b, preferred_element_type=F32)


def _dot_nt(a, b):
    return lax.dot_general(a, b, (((1,), (1,)), ((), ())), preferred_element_type=F32)


def _rms(x, g):
    return x * lax.rsqrt(jnp.mean(x * x, axis=-1, keepdims=True) + EPS) * g


def _count(m):
    return jnp.sum(jnp.where(m, 1.0, 0.0), axis=1, keepdims=True)


def _topk_select(x_ref, sel_ref, valid_chunk, k):
    R, C = x_ref.shape
    kf = jnp.float32(k)
    x = x_ref[...]
    lo = jnp.min(x, axis=1, keepdims=True)
    hi = jnp.max(x, axis=1, keepdims=True)
    cnt = jnp.full((R, 1), float(C), F32)

    def probe(state, pivot):
        lo, hi, cnt = state
        c = _count(x_ref[...] >= pivot)
        up = (c >= kf) & (pivot > lo)
        dn = (c < kf) & (pivot < hi)
        return jnp.where(up, pivot, lo), jnp.where(dn, pivot, hi), jnp.where(up, c, cnt)

    state = probe((lo, hi, cnt), hi)
    state = probe(state, jnp.full((R, 1), NEG * (1.0 - 2.0 ** -20), F32))
    state = probe(state, jnp.min(jnp.where(x > NEG, x, jnp.inf), axis=1, keepdims=True))
    state = probe(state, jnp.zeros((R, 1), F32))
    state = probe(state, jnp.full((R, 1), jnp.finfo(F32).tiny, F32))
    tied = jnp.where(_count(x > state[0]) < kf, 1.0, 0.0)
    state = state + (tied,)

    def midpoint(lo, hi):
        return 0.5 * lo + 0.5 * hi

    def n_open(state):
        lo, hi, cnt, tied = state
        mid = midpoint(lo, hi)
        return jnp.max(jnp.where((cnt > kf) & (tied == 0.0) & (mid > lo) & (mid < hi), 1.0, 0.0))

    def halve(carry):
        it, _, state = carry
        state = probe(state[:3], midpoint(state[0], state[1])) + state[3:]
        return it + 1, n_open(state), state

    def halve_or_step(carry):
        it, _, (lo, hi, cnt, tied) = carry
        xv = x_ref[...]
        nxt = jnp.min(jnp.where(xv > lo, xv, jnp.inf), axis=1, keepdims=True)
        near = cnt - kf <= SNAP_EXCESS
        pivot = jnp.where(near, nxt, midpoint(lo, hi))
        c = _count(xv >= pivot)
        up = (c >= kf) & (pivot > lo)
        dn = (c < kf) & (pivot < hi)
        tied = jnp.where(near & (c < kf), 1.0, tied)
        state = (jnp.where(up, pivot, lo), jnp.where(dn, pivot, hi), jnp.where(up, c, cnt), tied)
        return it + 1, n_open(state), state

    carry = (jnp.int32(0), n_open(state), state)
    carry = lax.while_loop(lambda c: (c[1] > 0.0) & (c[0] < HALVING_STEPS), halve, carry)
    _, _, (tau, _, _, _) = lax.while_loop(lambda c: (c[1] > 0.0) & (c[0] < 512), halve_or_step, carry)
    need = kf - _count(x > tau)
    ri = lax.broadcasted_iota(I32, (LANES, LANES), 0)
    ci = lax.broadcasted_iota(I32, (LANES, LANES), 1)
    upto = jnp.where(ri <= ci, 1.0, 0.0).astype(BF16)
    before = jnp.zeros((R, 1), F32)
    for c0 in range(0, C, LANES):
        xc = x_ref[:, c0:c0 + LANES]
        eq = xc == tau
        run = _dot(jnp.where(eq, 1.0, 0.0).astype(BF16), upto)
        take = (xc > tau) | (eq & (before + run <= need))
        sel_ref[:, c0:c0 + LANES] = jnp.where(take & valid_chunk(c0), 1.0, 0.0).astype(sel_ref.dtype)
        before = before + run[:, LANES - 1:LANES]


def _attn_proj_kernel(xp_ref, xs_ref, g_ref, w_ref, gq_ref, gk_ref, gki_ref,
                      q_ref, kb_ref, vb_ref, qi_ref, kw_ref, kp_ref, vp_ref, ks_ref, vs_ref, *, n_prompt_tiles):
    d = N_HEADS * HEAD_DIM
    is_prompt = pl.program_id(0) < n_prompt_tiles
    x = jnp.where(is_prompt, xp_ref[...], xs_ref[...])
    h = _rms(x, g_ref[...]).astype(BF16)
    scale = HEAD_DIM ** -0.5

    def head_norm(y, g):
        return y * lax.rsqrt(jnp.mean(y * y, axis=-1, keepdims=True) + EPS) * g

    yq = _dot(h, w_ref[:, 0:d])
    for hd in range(N_HEADS):
        sl = slice(hd * HEAD_DIM, (hd + 1) * HEAD_DIM)
        q_ref[:, sl] = (head_norm(yq[:, sl], gq_ref[...]) * scale).astype(BF16)
    yk = _dot(h, w_ref[:, d:2 * d])
    kn = [head_norm(yk[:, hd * HEAD_DIM:(hd + 1) * HEAD_DIM], gk_ref[...]) for hd in range(N_HEADS)]
    yv = _dot(h, w_ref[:, 2 * d:3 * d])

    @pl.when(is_prompt)
    def _():
        for hd in range(N_HEADS):
            kp_ref[:, hd * HEAD_DIM:(hd + 1) * HEAD_DIM] = kn[hd]
            kb_ref[:, hd * HEAD_DIM:(hd + 1) * HEAD_DIM] = kn[hd].astype(BF16)
        vp_ref[...] = yv
        vb_ref[...] = yv.astype(BF16)

    @pl.when(jnp.logical_not(is_prompt))
    def _():
        for hd in range(N_HEADS):
            ks_ref[:, hd * HEAD_DIM:(hd + 1) * HEAD_DIM] = kn[hd]
        vs_ref[...] = yv

    ni = IDX_HEADS * IDX_DIM
    qi_ref[...] = (_dot(h, w_ref[:, 3 * d:3 * d + ni]) * (IDX_DIM ** -0.5)).astype(BF16)
    yt = _dot(h, w_ref[:, 3 * d + ni:3 * d + ni + LANES])
    lane = lax.broadcasted_iota(I32, yt.shape, 1)
    is_ki = lane < IDX_DIM
    ms = jnp.sum(jnp.where(is_ki, yt * yt, 0.0), axis=-1, keepdims=True) * (1.0 / IDX_DIM)
    kw_ref[...] = jnp.where(is_ki, yt * lax.rsqrt(ms + EPS) * gki_ref[...], yt * (IDX_HEADS ** -0.5))


def _attn_project(x_p, x_s, g_mix, w_in, g_q, g_k, g_kidx, tm):
    n_p, D = x_p.shape
    n_s = x_s.shape[0]
    T = n_p + n_s
    npt = n_p // tm
    d = N_HEADS * HEAD_DIM
    n_in = w_in.shape[1]
    n_pad = 3 * d + IDX_HEADS * IDX_DIM + LANES
    w = jnp.pad(w_in, ((0, 0), (0, n_pad - n_in))).astype(BF16)
    gki = jnp.pad(g_kidx, (0, LANES - IDX_DIM)).reshape(1, LANES)
    row = lambda i: (i, 0)
    prow = lambda i: (jnp.minimum(i, npt - 1), 0)
    srow = lambda i: (jnp.maximum(i - npt, 0), 0)
    const = lambda i: (0, 0)
    stream = lambda n, dt: jax.ShapeDtypeStruct((T, n), dt)
    return pl.pallas_call(
        functools.partial(_attn_proj_kernel, n_prompt_tiles=npt),
        grid=(T // tm,),
        in_specs=[pl.BlockSpec((tm, D), prow), pl.BlockSpec((tm, D), srow), pl.BlockSpec((1, D), const),
                  pl.BlockSpec((D, n_pad), const), pl.BlockSpec((1, HEAD_DIM), const),
                  pl.BlockSpec((1, HEAD_DIM), const), pl.BlockSpec((1, LANES), const)],
        out_specs=[pl.BlockSpec((tm, d), row), pl.BlockSpec((tm, d), prow), pl.BlockSpec((tm, d), prow),
                   pl.BlockSpec((tm, IDX_HEADS * IDX_DIM), row), pl.BlockSpec((tm, LANES), row),
                   pl.BlockSpec((tm, d), prow), pl.BlockSpec((tm, d), prow),
                   pl.BlockSpec((tm, d), srow), pl.BlockSpec((tm, d), srow)],
        out_shape=[stream(d, BF16), jax.ShapeDtypeStruct((n_p, d), BF16), jax.ShapeDtypeStruct((n_p, d), BF16),
                   stream(IDX_HEADS * IDX_DIM, BF16), stream(LANES, F32),
                   jax.ShapeDtypeStruct((n_p, d), F32), jax.ShapeDtypeStruct((n_p, d), F32),
                   jax.ShapeDtypeStruct((n_s, d), F32), jax.ShapeDtypeStruct((n_s, d), F32)],
        compiler_params=_cparams("arbitrary"),
        name="attn_project",
    )(x_p, x_s, g_mix.reshape(1, D), w, g_q.reshape(1, HEAD_DIM), g_k.reshape(1, HEAD_DIM), gki)


def _prompt_attn_kernel(*refs, topk, t_first, aliased):
    if aliased:
        refs = refs[1:]
    q_ref, qi_ref, kwq_ref, kwk_ref, kb_ref, vb_ref, o_ref, x_ref, sel_ref = refs
    tq, S = x_ref.shape
    t0 = t_first + pl.program_id(1) * tq
    ki = kwk_ref[:, 0:IDX_DIM].astype(BF16)
    qi = qi_ref[...]
    score = None
    for h in range(IDX_HEADS):
        logit = _dot_nt(qi[:, h * IDX_DIM:(h + 1) * IDX_DIM], ki)
        term = jnp.maximum(logit, 0.0) * kwq_ref[:, IDX_DIM + h:IDX_DIM + h + 1]
        score = term if score is None else score + term
    row = lax.broadcasted_iota(I32, (tq, S), 0) + t0
    col = lax.broadcasted_iota(I32, (tq, S), 1)
    x_ref[...] = jnp.where(col <= row, score, NEG)
    rows_c = lax.broadcasted_iota(I32, (tq, LANES), 0) + t0
    cols_c = lax.broadcasted_iota(I32, (tq, LANES), 1)
    _topk_select(x_ref, sel_ref, lambda c0: cols_c + c0 <= rows_c, topk)
    for h in range(N_HEADS):
        sl = slice(h * HEAD_DIM, (h + 1) * HEAD_DIM)
        s = _dot_nt(q_ref[:, sl], kb_ref[:, sl])
        s = jnp.where(sel_ref[...] != 0.0, s, NEG)
        m = jnp.max(s, axis=-1, keepdims=True)
        p = jnp.exp(s - m)
        l = jnp.sum(p, axis=-1, keepdims=True)
        o = _dot(p.astype(BF16), vb_ref[:, sl]) / l
        o_ref[:, sl] = o.astype(o_ref.dtype)


def _prompt_attention(q, qi, kw, kb, vb, B, S, tq, topk):
    d = N_HEADS * HEAD_DIM
    nq = S // tq
    span = 2 if nq % 2 == 0 else 1
    kw3, kb3, vb3 = (a[:B * S].reshape(B, S, a.shape[1]) for a in (kw, kb, vb))
    o = None
    for j in range(nq // span):
        s_eff = (j + 1) * span * tq
        qrow = lambda b, i, j=j: (b * nq + j * span + i, 0)
        batch = lambda b, i: (b, 0, 0)
        in_specs = [pl.BlockSpec((tq, d), qrow), pl.BlockSpec((tq, IDX_HEADS * IDX_DIM), qrow),
                    pl.BlockSpec((tq, LANES), qrow), pl.BlockSpec((None, s_eff, LANES), batch),
                    pl.BlockSpec((None, s_eff, d), batch), pl.BlockSpec((None, s_eff, d), batch)]
        args = [q, qi, kw, kw3, kb3, vb3]
        aliases = {}
        if o is not None:
            in_specs = [pl.BlockSpec(memory_space=pl.ANY)] + in_specs
            args = [o] + args
            aliases = {0: 0}
        o = pl.pallas_call(
            functools.partial(_prompt_attn_kernel, topk=topk, t_first=j * span * tq, aliased=o is not None),
            grid=(B, span),
            in_specs=in_specs,
            out_specs=pl.BlockSpec((tq, d), qrow),
            out_shape=jax.ShapeDtypeStruct((B * S, d), BF16),
            scratch_shapes=[pltpu.VMEM((tq, s_eff), F32), pltpu.VMEM((tq, s_eff), F32)],
            input_output_aliases=aliases,
            compiler_params=_cparams("arbitrary", "arbitrary"),
            name=f"prompt_attention_{j}",
        )(*args)
    return o


def _sample_scores_kernel(pt_ref, qi_ref, w_ref, kin_ref, *rest, n_pages, n_new):
    page_refs = rest[:n_pages]
    s_ref = rest[n_pages]
    kall_ref = rest[n_pages + 1]
    past = n_pages * PAGE_SIZE
    C = kall_ref.shape[1]
    for p in range(n_pages):
        kall_ref[:, p * PAGE_SIZE:(p + 1) * PAGE_SIZE] = page_refs[p][...].astype(BF16)
    kall_ref[:, past:C] = kin_ref[...].astype(BF16)
    logit = _dot(qi_ref[...], kall_ref[...])
    term = jnp.maximum(logit, 0.0) * w_ref[...]
    col = lax.broadcasted_iota(I32, (1, C), 1)
    for t in range(n_new):
        s_t = jnp.sum(term[t * IDX_HEADS:(t + 1) * IDX_HEADS, :], axis=0, keepdims=True)
        s_ref[t:t + 1, :] = jnp.where(col <= past + t, s_t, NEG)


def _sample_scores(page_table, qi_s, wi_s, ki_new_t, kidx_t, C):
    Bd, n_pages = page_table.shape
    n_new = qi_s.shape[1] // IDX_HEADS
    page_spec = lambda p: pl.BlockSpec((None, IDX_DIM, PAGE_SIZE), lambda b, pt: (pt[b, p], 0, 0))
    per_seq = lambda b, pt: (b, 0, 0)
    grid_spec = pltpu.PrefetchScalarGridSpec(
        num_scalar_prefetch=1,
        grid=(Bd,),
        in_specs=[pl.BlockSpec((None, n_new * IDX_HEADS, IDX_DIM), per_seq),
                  pl.BlockSpec((None, n_new * IDX_HEADS, 1), per_seq),
                  pl.BlockSpec((None, IDX_DIM, C - n_pages * PAGE_SIZE), per_seq)]
                 + [page_spec(p) for p in range(n_pages)],
        out_specs=pl.BlockSpec((None, n_new, C), per_seq),
        scratch_shapes=[pltpu.VMEM((IDX_DIM, C), BF16)],
    )
    return pl.pallas_call(
        functools.partial(_sample_scores_kernel, n_pages=n_pages, n_new=n_new),
        grid_spec=grid_spec,
        out_shape=jax.ShapeDtypeStruct((Bd, n_new, C), F32),
        compiler_params=_cparams("arbitrary"),
        name="sample_scores",
    )(page_table, qi_s, wi_s, ki_new_t, *([kidx_t] * n_pages))


def _sample_topk_kernel(s_ref, selx_ref, seln_ref, x_ref, sel_ref, *, topk, past, n_new):
    R, C = x_ref.shape
    flat = PAGE_SIZE * N_HEADS
    x_ref[...] = s_ref[...]
    last_c = past + lax.broadcasted_iota(I32, (R, LANES), 0) % n_new
    cols_c = lax.broadcasted_iota(I32, (R, LANES), 1)
    _topk_select(x_ref, sel_ref, lambda c0: cols_c + c0 <= last_c, topk)
    sel = sel_ref[...]
    e_r = lax.broadcasted_iota(I32, (PAGE_SIZE, flat), 0)
    e_c = lax.broadcasted_iota(I32, (PAGE_SIZE, flat), 1)
    spread = jnp.where(e_c // N_HEADS == e_r, 1.0, 0.0).astype(BF16)
    for p in range(past // PAGE_SIZE):
        selx_ref[:, p * flat:(p + 1) * flat] = _dot(
            sel[:, p * PAGE_SIZE:(p + 1) * PAGE_SIZE], spread).astype(selx_ref.dtype)
    seln_ref[...] = sel[:, past:C]


def _sample_topk(scores, topk, past, n_new, tr):
    R, C = scores.shape
    wide = past * N_HEADS
    row = lambda i: (i, 0)
    return pl.pallas_call(
        functools.partial(_sample_topk_kernel, topk=topk, past=past, n_new=n_new),
        grid=(R // tr,),
        in_specs=[pl.BlockSpec((tr, C), row)],
        out_specs=[pl.BlockSpec((tr, wide), row), pl.BlockSpec((tr, C - past), row)],
        out_shape=[jax.ShapeDtypeStruct((R, wide), BF16), jax.ShapeDtypeStruct((R, C - past), BF16)],
        scratch_shapes=[pltpu.VMEM((tr, C), F32), pltpu.VMEM((tr, C), BF16)],
        compiler_params=_cparams("arbitrary"),
        name="sample_topk",
    )(scores)


def _sample_attn_kernel(pt_ref, q_ref, selx_ref, seln_ref, kn_ref, vn_ref, *rest, pg, n_new):
    k_refs = rest[:pg]
    v_refs = rest[pg:2 * pg]
    o_ref, m_ref, l_ref, acc_ref = rest[2 * pg:]
    g = pl.program_id(1)
    rows = n_new * N_HEADS
    flat = PAGE_SIZE * N_HEADS

    @pl.when(g == 0)
    def _():
        m_ref[...] = jnp.full(m_ref.shape, NEG, F32)
        l_ref[...] = jnp.zeros(l_ref.shape, F32)
        acc_ref[...] = jnp.zeros(acc_ref.shape, F32)

    q = q_ref[...]

    def expand_rows(x):
        return jnp.concatenate(
            [jnp.broadcast_to(x[t:t + 1, :], (N_HEADS, x.shape[1])) for t in range(n_new)], axis=0)

    def update(scores, keeps, values):
        scores = [jnp.where(kp, s, NEG) for s, kp in zip(scores, keeps)]
        m_old = m_ref[...]
        m_new = m_old
        for s in scores:
            m_new = jnp.maximum(m_new, jnp.max(s, axis=-1, keepdims=True))
        alpha = jnp.exp(m_old - m_new)
        l_new = alpha * l_ref[...]
        acc = alpha * acc_ref[...]
        for s, kp, v in zip(scores, keeps, values):
            p = jnp.where(kp, jnp.exp(s - m_new), 0.0)
            l_new = l_new + jnp.sum(p, axis=-1, keepdims=True)
            acc = acc + _dot(p.astype(BF16), v)
        m_ref[...] = m_new
        l_ref[...] = l_new
        acc_ref[...] = acc

    hrow = lax.broadcasted_iota(I32, (rows, flat), 0) % N_HEADS
    hcol = lax.broadcasted_iota(I32, (rows, flat), 1) % N_HEADS
    same_head = hrow == hcol
    scores, keeps, values = [], [], []
    for p in range(pg):
        scores.append(_dot_nt(q, k_refs[p][...].astype(BF16)))
        sel_p = selx_ref[:, p * flat:(p + 1) * flat].astype(F32)
        keeps.append(same_head & (expand_rows(sel_p) > 0.5))
        values.append(v_refs[p][...].astype(BF16))
    update(scores, keeps, values)

    @pl.when(g == pl.num_programs(1) - 1)
    def _():
        s = _dot_nt(q, kn_ref[...].astype(BF16))
        r2 = lax.broadcasted_iota(I32, (rows, rows), 0)
        c2 = lax.broadcasted_iota(I32, (rows, rows), 1)
        sn = seln_ref[...].astype(F32)
        seln = jnp.concatenate(
            [jnp.broadcast_to(sn[:, j:j + 1], (n_new, N_HEADS)) for j in range(n_new)], axis=1)
        keep = (r2 % N_HEADS == c2 % N_HEADS) & (expand_rows(seln) > 0.5)
        update([s], [keep], [vn_ref[...].astype(BF16)])
        o_ref[...] = (acc_ref[...] / l_ref[...]).astype(o_ref.dtype)


def _sample_attention(page_table, q_s, selx, seln, k_new, v_new, cache_k_l, cache_v_l, pg):
    Bd, n_pages = page_table.shape
    rows = q_s.shape[1]
    n_new = rows // N_HEADS
    flat = PAGE_SIZE * N_HEADS
    per_seq = lambda b, g, pt: (b, 0, 0)
    kv_spec = lambda p: pl.BlockSpec((None, flat, HEAD_DIM), lambda b, g, pt: (pt[b, g * pg + p], 0, 0))
    grid_spec = pltpu.PrefetchScalarGridSpec(
        num_scalar_prefetch=1,
        grid=(Bd, n_pages // pg),
        in_specs=[pl.BlockSpec((None, rows, HEAD_DIM), per_seq),
                  pl.BlockSpec((None, n_new, pg * flat), lambda b, g, pt: (b, 0, g)),
                  pl.BlockSpec((None, n_new, seln.shape[2]), per_seq),
                  pl.BlockSpec((None, rows, HEAD_DIM), per_seq),
                  pl.BlockSpec((None, rows, HEAD_DIM), per_seq)]
                 + [kv_spec(p) for p in range(pg)] + [kv_spec(p) for p in range(pg)],
        out_specs=pl.BlockSpec((None, rows, HEAD_DIM), per_seq),
        scratch_shapes=[pltpu.VMEM((rows, 1), F32), pltpu.VMEM((rows, 1), F32),
                        pltpu.VMEM((rows, HEAD_DIM), F32)],
    )
    return pl.pallas_call(
        functools.partial(_sample_attn_kernel, pg=pg, n_new=n_new),
        grid_spec=grid_spec,
        out_shape=jax.ShapeDtypeStruct((Bd, rows, HEAD_DIM), BF16),
        compiler_params=_cparams("arbitrary", "arbitrary"),
        name="sample_attention",
    )(page_table, q_s, selx, seln, k_new, v_new, *([cache_k_l] * pg), *([cache_v_l] * pg))


def _out_proj_kernel(op_ref, os_ref, w_ref, xp_ref, xs_ref, y_ref, *, n_prompt_tiles):
    is_prompt = pl.program_id(0) < n_prompt_tiles
    o = jnp.where(is_prompt, op_ref[...], os_ref[...])
    x = jnp.where(is_prompt, xp_ref[...], xs_ref[...])
    y_ref[...] = x + _dot(o, w_ref[...])


def _out_proj(o_p, o_s, w, x_p, x_s, tm):
    n_p, D = x_p.shape
    T = n_p + x_s.shape[0]
    npt = n_p // tm
    prow = lambda i: (jnp.minimum(i, npt - 1), 0)
    srow = lambda i: (jnp.maximum(i - npt, 0), 0)
    return pl.pallas_call(
        functools.partial(_out_proj_kernel, n_prompt_tiles=npt),
        grid=(T // tm,),
        in_specs=[pl.BlockSpec((tm, o_p.shape[1]), prow), pl.BlockSpec((tm, o_s.shape[1]), srow),
                  pl.BlockSpec(w.shape, lambda i: (0, 0)),
                  pl.BlockSpec((tm, D), prow), pl.BlockSpec((tm, D), srow)],
        out_specs=pl.BlockSpec((tm, D), lambda i: (i, 0)),
        out_shape=jax.ShapeDtypeStruct((T, D), F32),
        compiler_params=_cparams("arbitrary"),
        name="attn_out_proj",
    )(o_p, o_s, w.astype(BF16), x_p, x_s)


def _ffn_kernel(x_ref, g_ref, wg_ref, wu_ref, wd_ref, y_ref, h_ref):
    f = pl.program_id(1)

    @pl.when(f == 0)
    def _():
        x = x_ref[...]
        h_ref[...] = _rms(x, g_ref[...]).astype(BF16)
        y_ref[...] = x

    h = h_ref[...]
    a = jax.nn.silu(_dot(h, wg_ref[...])) * _dot(h, wu_ref[...])
    y_ref[...] += _dot(a.astype(BF16), wd_ref[...])


def _ffn(x, g, w_gate, w_up, w_down, tm, tf):
    T, D = x.shape
    F = w_gate.shape[1]
    row = lambda i, f: (i, 0)
    return pl.pallas_call(
        _ffn_kernel,
        grid=(T // tm, F // tf),
        in_specs=[pl.BlockSpec((tm, D), row), pl.BlockSpec((1, D), lambda i, f: (0, 0)),
                  pl.BlockSpec((D, tf), lambda i, f: (0, f)), pl.BlockSpec((D, tf), lambda i, f: (0, f)),
                  pl.BlockSpec((tf, D), lambda i, f: (f, 0))],
        out_specs=pl.BlockSpec((tm, D), row),
        out_shape=jax.ShapeDtypeStruct((T, D), F32),
        scratch_shapes=[pltpu.VMEM((tm, D), BF16)],
        compiler_params=_cparams("arbitrary", "arbitrary"),
        name="dense_swiglu",
    )(x, g.reshape(1, D), w_gate.astype(BF16), w_up.astype(BF16), w_down.astype(BF16))


def _gmlp_kernel(x_ref, g_ref, win_ref, lng_ref, lnb_ref, ws_ref, bs_ref, wout_ref, y_ref, v_ref):
    tm, D = x_ref.shape
    W = lng_ref.shape[1]
    gd = W // GM_GROUPS
    x = x_ref[...]
    h = _rms(x, g_ref[...]).astype(BF16)
    z = _dot(h, win_ref[...])
    uv = 0.5 * z * (1.0 + lax.erf(z * math.sqrt(0.5)))
    u = uv[:, :W]
    v = uv[:, W:]
    mu = jnp.mean(v, axis=-1, keepdims=True)
    var = jnp.mean(jnp.square(v - mu), axis=-1, keepdims=True)
    v = (v - mu) * lax.rsqrt(var + EPS) * lng_ref[...] + lnb_ref[...]
    v_ref[...] = v
    vb = v.astype(BF16)
    ri = lax.broadcasted_iota(I32, (GM_CHUNK, GM_CHUNK), 0)
    ci = lax.broadcasted_iota(I32, (GM_CHUNK, GM_CHUNK), 1)
    causal = ci <= ri
    ws = [jnp.where(causal, ws_ref[g], 0.0).astype(BF16) for g in range(GM_GROUPS)]
    chunks = []
    for c in range(tm // GM_CHUNK):
        rs = slice(c * GM_CHUNK, (c + 1) * GM_CHUNK)
        mixed = jnp.concatenate(
            [_dot(ws[g], vb[rs, g * gd:(g + 1) * gd]) for g in range(GM_GROUPS)], axis=1)
        chunks.append(u[rs, :] * (mixed + bs_ref[...]))
    gated = jnp.concatenate(chunks, axis=0).astype(BF16)
    y_ref[...] = x + _dot(gated, wout_ref[...])


def _gmlp(x, g, w_in, ln_g, ln_b, w_s, b_s, w_out, tm, n_prompt_tiles, n_new):
    T, D = x.shape
    W = ln_g.shape[0]
    gd = W // GM_GROUPS
    eye = jnp.eye(GM_CHUNK // n_new, dtype=w_s.dtype)
    ws_sample = jax.vmap(lambda m: jnp.kron(eye, m))(w_s[:, :n_new, :n_new])
    ws_all = jnp.stack([w_s, ws_sample])
    bs_prompt = jnp.repeat(b_s.T, gd, axis=1)
    bs_sample = jnp.tile(bs_prompt[:n_new], (GM_CHUNK // n_new, 1))
    bs_all = jnp.stack([bs_prompt, bs_sample])
    kind = lambda i: jnp.where(i >= n_prompt_tiles, 1, 0)
    row = lambda i: (i, 0)
    const = lambda i: (0, 0)
    n_tiles = T // tm
    return pl.pallas_call(
        _gmlp_kernel,
        grid=(n_tiles,),
        in_specs=[pl.BlockSpec((tm, D), row), pl.BlockSpec((1, D), const),
                  pl.BlockSpec((D, 2 * W), const), pl.BlockSpec((1, W), const), pl.BlockSpec((1, W), const),
                  pl.BlockSpec((None, GM_GROUPS, GM_CHUNK, GM_CHUNK), lambda i: (kind(i), 0, 0, 0)),
                  pl.BlockSpec((None, GM_CHUNK, W), lambda i: (kind(i), 0, 0)),
                  pl.BlockSpec((W, D), const)],
        out_specs=[pl.BlockSpec((tm, D), row),
                   pl.BlockSpec((tm, W), lambda i: (jnp.maximum(i - n_prompt_tiles, 0), 0))],
        out_shape=[jax.ShapeDtypeStruct((T, D), F32),
                   jax.ShapeDtypeStruct(((n_tiles - n_prompt_tiles) * tm, W), F32)],
        compiler_params=_cparams("arbitrary"),
        name="gmlp_mixer",
    )(x, g.reshape(1, D), w_in.astype(BF16), ln_g.reshape(1, W), ln_b.reshape(1, W),
      ws_all, bs_all, w_out.astype(BF16))


def _router_kernel(x_ref, g_ref, wr_ref, h_ref, comb_ref, rank_ref, rankt_ref):
    tm = x_ref.shape[0]
    hf = _rms(x_ref[...], g_ref[...])
    h_ref[...] = hf.astype(BF16)
    logits = jnp.dot(hf, wr_ref[...], preferred_element_type=F32, precision=lax.Precision.HIGHEST)
    lane = lax.broadcasted_iota(I32, logits.shape, 1)
    logits = jnp.where(lane < N_EXPERTS, logits, -jnp.inf)
    m1 = jnp.max(logits, axis=-1, keepdims=True)
    i1 = jnp.min(jnp.where(logits == m1, lane, LANES), axis=-1, keepdims=True)
    rest = jnp.where(lane == i1, -jnp.inf, logits)
    m2 = jnp.max(rest, axis=-1, keepdims=True)
    i2 = jnp.min(jnp.where(rest == m2, lane, LANES), axis=-1, keepdims=True)
    e2 = jnp.exp(m2 - m1)
    den = 1.0 + e2
    comb_ref[...] = jnp.where(lane == i1, 1.0 / den, 0.0) + jnp.where(lane == i2, e2 / den, 0.0)
    routed = jnp.where((lane == i1) | (lane == i2), 1.0, 0.0).astype(BF16)
    r = lax.broadcasted_iota(I32, (tm, tm), 0)
    c = lax.broadcasted_iota(I32, (tm, tm), 1)
    rank = _dot(jnp.where(c < r, 1.0, 0.0).astype(BF16), routed)
    rank = jnp.where(routed > 0, rank, -1.0)
    rank_ref[...] = rank
    rankt_ref[...] = rank.T


def _router(x, g, w_router, tm):
    T, D = x.shape
    wr = jnp.pad(w_router, ((0, 0), (0, LANES - N_EXPERTS)))
    row = lambda i: (i, 0)
    colb = lambda i: (0, i)
    const = lambda i: (0, 0)
    return pl.pallas_call(
        _router_kernel,
        grid=(T // tm,),
        in_specs=[pl.BlockSpec((tm, D), row), pl.BlockSpec((1, D), const), pl.BlockSpec((D, LANES), const)],
        out_specs=[pl.BlockSpec((tm, D), row), pl.BlockSpec((tm, LANES), row),
                   pl.BlockSpec((tm, LANES), row), pl.BlockSpec((LANES, tm), colb)],
        out_shape=[jax.ShapeDtypeStruct((T, D), BF16), jax.ShapeDtypeStruct((T, LANES), F32),
                   jax.ShapeDtypeStruct((T, LANES), F32), jax.ShapeDtypeStruct((LANES, T), F32)],
        compiler_params=_cparams("arbitrary"),
        name="moe_router",
    )(x, g.reshape(1, D), wr)


def _moe_kernel(cnt_ref, h_ref, comb_ref, rank_ref, rankt_ref, wg_ref, wu_ref, wd_ref, acc_ref, *rest,
                e, n_prompt_tiles, n_chunks, n_out):
    outs = rest[:n_out]
    wg_s, wu_s, wd_s = rest[n_out:n_out + 3]
    y_ref = outs[0] if n_out == 1 else rest[n_out + 3]
    s = pl.program_id(0)
    tm = h_ref.shape[0]

    tf = wg_ref.shape[1]
    for c in range(n_chunks):
        @pl.when(s == c)
        def _(c=c):
            wg_s[:, c * tf:(c + 1) * tf] = wg_ref[...].astype(BF16)
            wu_s[:, c * tf:(c + 1) * tf] = wu_ref[...].astype(BF16)
            wd_s[c * tf:(c + 1) * tf, :] = wd_ref[...].astype(BF16)

    @pl.when(s >= n_chunks - 1)
    def _():
        i = s - (n_chunks - 1)
        gate = comb_ref[:, e:e + 1]
        rank_c = rank_ref[:, e:e + 1]
        rank_r = rankt_ref[e:e + 1, :]
        n = cnt_ref[i]

        def block(first, rows):
            base = first.astype(F32)
            slot_r = lax.broadcasted_iota(I32, (rows, tm), 0).astype(F32) + base
            take = jnp.where(rank_r == slot_r, 1.0, 0.0).astype(BF16)
            xs = _dot(take, h_ref[...]).astype(BF16)
            a = jax.nn.silu(_dot(xs, wg_s[...])) * _dot(xs, wu_s[...])
            y = _dot(a.astype(BF16), wd_s[...])
            slot_c = lax.broadcasted_iota(I32, (tm, rows), 1).astype(F32) + base
            put = jnp.where(rank_c == slot_c, 1.0, 0.0).astype(BF16)
            y_ref[...] += gate * _dot(put, y.astype(BF16))

        y_ref[...] = acc_ref[...]
        smaller = 0
        for rows in MOE_BLOCK_ROWS:
            @pl.when((n > smaller) & (n <= rows))
            def _(rows=rows):
                block(jnp.int32(0), rows)
            smaller = rows

        @pl.when(n > smaller)
        def _():
            def step(jb, carry):
                block(jb * MOE_BLOCK_ROWS[0], MOE_BLOCK_ROWS[0])
                return carry
            lax.fori_loop(0, (n + MOE_BLOCK_ROWS[0] - 1) // MOE_BLOCK_ROWS[0], step, 0)

        if n_out > 1:
            @pl.when(i < n_prompt_tiles)
            def _():
                outs[0][...] = y_ref[...]

            @pl.when(i >= n_prompt_tiles)
            def _():
                outs[1][...] = y_ref[...]


def _moe_expert(e, acc, h, comb, rank, rankt, counts_e, w_gate, w_up, w_down, tm, n_prompt_tiles, last):
    T, D = acc.shape
    F = w_gate.shape[2]
    tf = _largest_divisor(F, (512, 256, 128))
    nc = F // tf
    npt = n_prompt_tiles
    tile = lambda s: jnp.maximum(s - (nc - 1), 0)
    chunk = lambda s: jnp.minimum(s, nc - 1)
    tok = lambda s, cnt: (tile(s), 0)
    if last:
        out_specs = [pl.BlockSpec((tm, D), lambda s, cnt: (jnp.minimum(tile(s), npt - 1), 0)),
                     pl.BlockSpec((tm, D), lambda s, cnt: (jnp.maximum(tile(s) - npt, 0), 0))]
        out_shape = [jax.ShapeDtypeStruct((npt * tm, D), F32), jax.ShapeDtypeStruct((T - npt * tm, D), F32)]
        aliases = {}
    else:
        out_specs = pl.BlockSpec((tm, D), tok)
        out_shape = jax.ShapeDtypeStruct((T, D), F32)
        aliases = {8: 0}
    grid_spec = pltpu.PrefetchScalarGridSpec(
        num_scalar_prefetch=1,
        grid=(T // tm + nc - 1,),
        in_specs=[pl.BlockSpec((tm, D), tok), pl.BlockSpec((tm, LANES), tok), pl.BlockSpec((tm, LANES), tok),
                  pl.BlockSpec((LANES, tm), lambda s, cnt: (0, tile(s))),
                  pl.BlockSpec((None, D, tf), lambda s, cnt: (e, 0, chunk(s))),
                  pl.BlockSpec((None, D, tf), lambda s, cnt: (e, 0, chunk(s))),
                  pl.BlockSpec((None, tf, D), lambda s, cnt: (e, chunk(s), 0)),
                  pl.BlockSpec((tm, D), tok)],
        out_specs=out_specs,
        scratch_shapes=[pltpu.VMEM((D, F), BF16), pltpu.VMEM((D, F), BF16),
                        pltpu.VMEM((F, D), BF16)] + ([pltpu.VMEM((tm, D), F32)] if last else []),
    )
    return pl.pallas_call(
        functools.partial(_moe_kernel, e=e, n_prompt_tiles=npt, n_chunks=nc, n_out=2 if last else 1),
        grid_spec=grid_spec,
        out_shape=out_shape,
        input_output_aliases=aliases,
        compiler_params=_cparams("arbitrary"),
        name=f"moe_expert_{e}",
    )(counts_e, h, comb, rank, rankt, w_gate, w_up, w_down, acc)


def _token_tile(n_prompt, n_sample):
    for tm in (512, 256, 128):
        if n_prompt % tm == 0 and n_sample % tm == 0:
            return tm
    raise ValueError("token counts must be multiples of 128")


def _largest_divisor(n, candidates):
    for c in candidates:
        if n % c == 0:
            return c
    return n


def kernel(x_prompt, x_sample, cache_k, cache_v, cache_kidx, page_table, norm_mix, norm_ffn, attn_w_in, attn_g_q, attn_g_k, attn_g_kidx, attn_w_out, gm_w_in, gm_ln_g, gm_ln_b, gm_w_s, gm_b_s, gm_w_out, ffn_w_gate, ffn_w_up, ffn_w_down, moe_w_router, moe_w_gate, moe_w_up, moe_w_down):
    B, S, D = x_prompt.shape
    Bd, n_new, _ = x_sample.shape
    n_pages = page_table.shape[1]
    past = n_pages * PAGE_SIZE
    assert norm_mix.shape[0] == 2 and D == N_HEADS * HEAD_DIM
    n_p, n_s = B * S, Bd * n_new
    T = n_p + n_s
    tm = _token_tile(n_p, n_s)
    npt = n_p // tm
    topk_prompt = min(IDX_TOPK, S // 4)
    topk_sample = min(IDX_TOPK, (past + n_new) // 4)
    d = N_HEADS * HEAD_DIM
    x_p = x_prompt.reshape(n_p, D)
    x_s = x_sample.reshape(n_s, D)

    q, kb, vb, qi, kw, k_p, v_p, k_s, v_s = _attn_project(
        x_p, x_s, norm_mix[0], attn_w_in[0], attn_g_q[0], attn_g_k[0], attn_g_kidx[0], tm)
    tq = _largest_divisor(S, (256, 128, 64))
    o_p = _prompt_attention(q, qi, kw, kb, vb, B, S, tq, topk_prompt)

    C = past + LANES
    qi_s = qi[n_p:].reshape(Bd, n_new * IDX_HEADS, IDX_DIM)
    wi_s = kw[n_p:, IDX_DIM:IDX_DIM + IDX_HEADS].reshape(Bd, n_new * IDX_HEADS, 1)
    ki_new_t = jnp.pad(jnp.swapaxes(kw[n_p:, :IDX_DIM].reshape(Bd, n_new, IDX_DIM), 1, 2),
                       ((0, 0), (0, 0), (0, LANES - n_new)))
    scores = _sample_scores(page_table, qi_s, wi_s, ki_new_t, jnp.swapaxes(cache_kidx[0], 1, 2), C)
    selx, seln = _sample_topk(scores.reshape(n_s, C), topk_sample, past, n_new,
                              _largest_divisor(n_s, (128, 64, 32, 16)))
    flat = PAGE_SIZE * N_HEADS
    n_pool = cache_k.shape[1]
    pg = _largest_divisor(n_pages, tuple(range(MAX_PAGES_PER_STEP, 0, -1)))
    o_s = _sample_attention(
        page_table, q[n_p:].reshape(Bd, n_new * N_HEADS, HEAD_DIM),
        selx.reshape(Bd, n_new, past * N_HEADS), seln.reshape(Bd, n_new, LANES),
        k_s.reshape(Bd, n_new * N_HEADS, HEAD_DIM), v_s.reshape(Bd, n_new * N_HEADS, HEAD_DIM),
        cache_k[0].reshape(n_pool, flat, HEAD_DIM), cache_v[0].reshape(n_pool, flat, HEAD_DIM), pg)
    x = _out_proj(o_p, o_s.reshape(n_s, d), attn_w_out[0], x_p, x_s, tm)
    x = _ffn(x, norm_ffn[0], ffn_w_gate[0], ffn_w_up[0], ffn_w_down[0], tm,
             _largest_divisor(ffn_w_gate.shape[2], (1408, 1024, 512, 256, 128)))

    x, gm_v = _gmlp(x, norm_mix[1], gm_w_in[0], gm_ln_g[0], gm_ln_b[0], gm_w_s[0], gm_b_s[0], gm_w_out[0],
                    tm, npt, n_new)
    h, comb, rank, rankt = _router(x, norm_ffn[1], moe_w_router[0], tm)
    counts = jnp.sum((rankt[:N_EXPERTS] >= 0.0).reshape(N_EXPERTS, T // tm, tm), axis=-1).astype(I32)
    wg, wu, wd = moe_w_gate[0], moe_w_up[0], moe_w_down[0]
    for e in range(N_EXPERTS):
        x = _moe_expert(e, x, h, comb, rank, rankt, counts[e], wg, wu, wd, tm, npt, e == N_EXPERTS - 1)
    y_p, y_s = x

    n_pp = n_p // PAGE_SIZE
    return (y_p.reshape(B, S, D), y_s.reshape(Bd, n_new, D),
            k_p.reshape(1, n_pp, PAGE_SIZE, N_HEADS, HEAD_DIM),
            v_p.reshape(1, n_pp, PAGE_SIZE, N_HEADS, HEAD_DIM),
            kw[:n_p, :IDX_DIM].reshape(1, n_pp, PAGE_SIZE, IDX_DIM),
            k_s.reshape(1, Bd, n_new, N_HEADS, HEAD_DIM),
            v_s.reshape(1, Bd, n_new, N_HEADS, HEAD_DIM),
            kw[n_p:, :IDX_DIM].reshape(1, Bd, n_new, IDX_DIM),
            gm_v.reshape(1, Bd, n_new, gm_ln_g.shape[1]))
```

```python
import functools
import math

import jax
import jax.numpy as jnp
from jax import lax
from jax.experimental import pallas as pl
from jax.experimental.pallas import tpu as pltpu

F32 = jnp.float32
BF16 = jnp.bfloat16
I32 = jnp.int32

N_HEADS = 8
HEAD_DIM = 128
IDX_HEADS = 8
IDX_DIM = 64
IDX_TOPK = 256
PAGE_SIZE = 128
GM_CHUNK = 128
GM_GROUPS = 8
N_EXPERTS = 8
EPS = 1e-6
NEG = -1e30

LANES = 128
MOE_BLOCK_ROWS = (128, 144, 160, 176, 192)
MAX_PAGES_PER_STEP = 16
HALVING_STEPS = 12
SNAP_EXCESS = 2.0
VMEM_LIMIT = 56 * 1024 * 1024


def _cparams(*sem):
    return pltpu.CompilerParams(dimension_semantics=sem, vmem_limit_bytes=VMEM_LIMIT)


def _dot(a, b):
    return jnp.dot(a, b, preferred_element_type=F32)


def _dot_nt(a, b):
    return lax.dot_general(a, b, (((1,), (1,)), ((), ())), preferred_element_type=F32)


def _rms(x, g):
    return x * lax.rsqrt(jnp.mean(x * x, axis=-1, keepdims=True) + EPS) * g


def _count(m):
    return jnp.sum(jnp.where(m, 1.0, 0.0), axis=1, keepdims=True)


def _topk_select(x_ref, sel_ref, valid_chunk, k):
    R, C = x_ref.shape
    kf = jnp.float32(k)
    x = x_ref[...]
    lo = jnp.min(x, axis=1, keepdims=True)
    hi = jnp.max(x, axis=1, keepdims=True)
    cnt = jnp.full((R, 1), float(C), F32)

    def probe(state, pivot):
        lo, hi, cnt = state
        c = _count(x_ref[...] >= pivot)
        up = (c >= kf) & (pivot > lo)
        dn = (c < kf) & (pivot < hi)
        return jnp.where(up, pivot, lo), jnp.where(dn, pivot, hi), jnp.where(up, c, cnt)

    state = probe((lo, hi, cnt), hi)
    state = probe(state, jnp.full((R, 1), NEG * (1.0 - 2.0 ** -20), F32))
    state = probe(state, jnp.min(jnp.where(x > NEG, x, jnp.inf), axis=1, keepdims=True))
    state = probe(state, jnp.zeros((R, 1), F32))
    state = probe(state, jnp.full((R, 1), jnp.finfo(F32).tiny, F32))
    tied = jnp.where(_count(x > state[0]) < kf, 1.0, 0.0)
    state = state + (tied,)

    def midpoint(lo, hi):
        return 0.5 * lo + 0.5 * hi

    def n_open(state):
        lo, hi, cnt, tied = state
        mid = midpoint(lo, hi)
        return jnp.max(jnp.where((cnt > kf) & (tied == 0.0) & (mid > lo) & (mid < hi), 1.0, 0.0))

    def halve(carry):
        it, _, state = carry
        state = probe(state[:3], midpoint(state[0], state[1])) + state[3:]
        return it + 1, n_open(state), state

    def halve_or_step(carry):
        it, _, (lo, hi, cnt, tied) = carry
        xv = x_ref[...]
        nxt = jnp.min(jnp.where(xv > lo, xv, jnp.inf), axis=1, keepdims=True)
        near = cnt - kf <= SNAP_EXCESS
        pivot = jnp.where(near, nxt, midpoint(lo, hi))
        c = _count(xv >= pivot)
        up = (c >= kf) & (pivot > lo)
        dn = (c < kf) & (pivot < hi)
        tied = jnp.where(near & (c < kf), 1.0, tied)
        state = (jnp.where(up, pivot, lo), jnp.where(dn, pivot, hi), jnp.where(up, c, cnt), tied)
        return it + 1, n_open(state), state

    carry = (jnp.int32(0), n_open(state), state)
    carry = lax.while_loop(lambda c: (c[1] > 0.0) & (c[0] < HALVING_STEPS), halve, carry)
    _, _, (tau, _, _, _) = lax.while_loop(lambda c: (c[1] > 0.0) & (c[0] < 512), halve_or_step, carry)
    need = kf - _count(x > tau)
    ri = lax.broadcasted_iota(I32, (LANES, LANES), 0)
    ci = lax.broadcasted_iota(I32, (LANES, LANES), 1)
    upto = jnp.where(ri <= ci, 1.0, 0.0).astype(BF16)
    before = jnp.zeros((R, 1), F32)
    for c0 in range(0, C, LANES):
        xc = x_ref[:, c0:c0 + LANES]
        eq = xc == tau
        run = _dot(jnp.where(eq, 1.0, 0.0).astype(BF16), upto)
        take = (xc > tau) | (eq & (before + run <= need))
        sel_ref[:, c0:c0 + LANES] = jnp.where(take & valid_chunk(c0), 1.0, 0.0).astype(sel_ref.dtype)
        before = before + run[:, LANES - 1:LANES]


def _attn_proj_kernel(xp_ref, xs_ref, g_ref, w_ref, gq_ref, gk_ref, gki_ref,
                      q_ref, kb_ref, vb_ref, qi_ref, kw_ref, kp_ref, vp_ref, ks_ref, vs_ref, *, n_prompt_tiles):
    d = N_HEADS * HEAD_DIM
    is_prompt = pl.program_id(0) < n_prompt_tiles
    x = jnp.where(is_prompt, xp_ref[...], xs_ref[...])
    h = _rms(x, g_ref[...]).astype(BF16)
    scale = HEAD_DIM ** -0.5

    def head_norm(y, g):
        return y * lax.rsqrt(jnp.mean(y * y, axis=-1, keepdims=True) + EPS) * g

    yq = _dot(h, w_ref[:, 0:d])
    for hd in range(N_HEADS):
        sl = slice(hd * HEAD_DIM, (hd + 1) * HEAD_DIM)
        q_ref[:, sl] = (head_norm(yq[:, sl], gq_ref[...]) * scale).astype(BF16)
    yk = _dot(h, w_ref[:, d:2 * d])
    kn = [head_norm(yk[:, hd * HEAD_DIM:(hd + 1) * HEAD_DIM], gk_ref[...]) for hd in range(N_HEADS)]
    yv = _dot(h, w_ref[:, 2 * d:3 * d])

    @pl.when(is_prompt)
    def _():
        for hd in range(N_HEADS):
            kp_ref[:, hd * HEAD_DIM:(hd + 1) * HEAD_DIM] = kn[hd]
            kb_ref[:, hd * HEAD_DIM:(hd + 1) * HEAD_DIM] = kn[hd].astype(BF16)
        vp_ref[...] = yv
        vb_ref[...] = yv.astype(BF16)

    @pl.when(jnp.logical_not(is_prompt))
    def _():
        for hd in range(N_HEADS):
            ks_ref[:, hd * HEAD_DIM:(hd + 1) * HEAD_DIM] = kn[hd]
        vs_ref[...] = yv

    ni = IDX_HEADS * IDX_DIM
    qi_ref[...] = (_dot(h, w_ref[:, 3 * d:3 * d + ni]) * (IDX_DIM ** -0.5)).astype(BF16)
    yt = _dot(h, w_ref[:, 3 * d + ni:3 * d + ni + LANES])
    lane = lax.broadcasted_iota(I32, yt.shape, 1)
    is_ki = lane < IDX_DIM
    ms = jnp.sum(jnp.where(is_ki, yt * yt, 0.0), axis=-1, keepdims=True) * (1.0 / IDX_DIM)
    kw_ref[...] = jnp.where(is_ki, yt * lax.rsqrt(ms + EPS) * gki_ref[...], yt * (IDX_HEADS ** -0.5))


def _attn_project(x_p, x_s, g_mix, w_in, g_q, g_k, g_kidx, tm):
    n_p, D = x_p.shape
    n_s = x_s.shape[0]
    T = n_p + n_s
    npt = n_p // tm
    d = N_HEADS * HEAD_DIM
    n_in = w_in.shape[1]
    n_pad = 3 * d + IDX_HEADS * IDX_DIM + LANES
    w = jnp.pad(w_in, ((0, 0), (0, n_pad - n_in))).astype(BF16)
    gki = jnp.pad(g_kidx, (0, LANES - IDX_DIM)).reshape(1, LANES)
    row = lambda i: (i, 0)
    prow = lambda i: (jnp.minimum(i, npt - 1), 0)
    srow = lambda i: (jnp.maximum(i - npt, 0), 0)
    const = lambda i: (0, 0)
    stream = lambda n, dt: jax.ShapeDtypeStruct((T, n), dt)
    return pl.pallas_call(
        functools.partial(_attn_proj_kernel, n_prompt_tiles=npt),
        grid=(T // tm,),
        in_specs=[pl.BlockSpec((tm, D), prow), pl.BlockSpec((tm, D), srow), pl.BlockSpec((1, D), const),
                  pl.BlockSpec((D, n_pad), const), pl.BlockSpec((1, HEAD_DIM), const),
                  pl.BlockSpec((1, HEAD_DIM), const), pl.BlockSpec((1, LANES), const)],
        out_specs=[pl.BlockSpec((tm, d), row), pl.BlockSpec((tm, d), prow), pl.BlockSpec((tm, d), prow),
                   pl.BlockSpec((tm, IDX_HEADS * IDX_DIM), row), pl.BlockSpec((tm, LANES), row),
                   pl.BlockSpec((tm, d), prow), pl.BlockSpec((tm, d), prow),
                   pl.BlockSpec((tm, d), srow), pl.BlockSpec((tm, d), srow)],
        out_shape=[stream(d, BF16), jax.ShapeDtypeStruct((n_p, d), BF16), jax.ShapeDtypeStruct((n_p, d), BF16),
                   stream(IDX_HEADS * IDX_DIM, BF16), stream(LANES, F32),
                   jax.ShapeDtypeStruct((n_p, d), F32), jax.ShapeDtypeStruct((n_p, d), F32),
                   jax.ShapeDtypeStruct((n_s, d), F32), jax.ShapeDtypeStruct((n_s, d), F32)],
        compiler_params=_cparams("arbitrary"),
        name="attn_project",
    )(x_p, x_s, g_mix.reshape(1, D), w, g_q.reshape(1, HEAD_DIM), g_k.reshape(1, HEAD_DIM), gki)


def _prompt_attn_kernel(*refs, topk, t_first, aliased):
    if aliased:
        refs = refs[1:]
    q_ref, qi_ref, kwq_ref, kwk_ref, kb_ref, vb_ref, o_ref, x_ref, sel_ref = refs
    tq, S = x_ref.shape
    t0 = t_first + pl.program_id(1) * tq
    ki = kwk_ref[:, 0:IDX_DIM].astype(BF16)
    qi = qi_ref[...]
    score = None
    for h in range(IDX_HEADS):
        logit = _dot_nt(qi[:, h * IDX_DIM:(h + 1) * IDX_DIM], ki)
        term = jnp.maximum(logit, 0.0) * kwq_ref[:, IDX_DIM + h:IDX_DIM + h + 1]
        score = term if score is None else score + term
    row = lax.broadcasted_iota(I32, (tq, S), 0) + t0
    col = lax.broadcasted_iota(I32, (tq, S), 1)
    x_ref[...] = jnp.where(col <= row, score, NEG)
    rows_c = lax.broadcasted_iota(I32, (tq, LANES), 0) + t0
    cols_c = lax.broadcasted_iota(I32, (tq, LANES), 1)
    _topk_select(x_ref, sel_ref, lambda c0: cols_c + c0 <= rows_c, topk)
    for h in range(N_HEADS):
        sl = slice(h * HEAD_DIM, (h + 1) * HEAD_DIM)
        s = _dot_nt(q_ref[:, sl], kb_ref[:, sl])
        s = jnp.where(sel_ref[...] != 0.0, s, NEG)
        m = jnp.max(s, axis=-1, keepdims=True)
        p = jnp.exp(s - m)
        l = jnp.sum(p, axis=-1, keepdims=True)
        o = _dot(p.astype(BF16), vb_ref[:, sl]) / l
        o_ref[:, sl] = o.astype(o_ref.dtype)


def _prompt_attention(q, qi, kw, kb, vb, B, S, tq, topk):
    d = N_HEADS * HEAD_DIM
    nq = S // tq
    span = 2 if nq % 2 == 0 else 1
    kw3, kb3, vb3 = (a[:B * S].reshape(B, S, a.shape[1]) for a in (kw, kb, vb))
    o = None
    for j in range(nq // span):
        s_eff = (j + 1) * span * tq
        qrow = lambda b, i, j=j: (b * nq + j * span + i, 0)
        batch = lambda b, i: (b, 0, 0)
        in_specs = [pl.BlockSpec((tq, d), qrow), pl.BlockSpec((tq, IDX_HEADS * IDX_DIM), qrow),
                    pl.BlockSpec((tq, LANES), qrow), pl.BlockSpec((None, s_eff, LANES), batch),
                    pl.BlockSpec((None, s_eff, d), batch), pl.BlockSpec((None, s_eff, d), batch)]
        args = [q, qi, kw, kw3, kb3, vb3]
        aliases = {}
        if o is not None:
            in_specs = [pl.BlockSpec(memory_space=pl.ANY)] + in_specs
            args = [o] + args
            aliases = {0: 0}
        o = pl.pallas_call(
            functools.partial(_prompt_attn_kernel, topk=topk, t_first=j * span * tq, aliased=o is not None),
            grid=(B, span),
            in_specs=in_specs,
            out_specs=pl.BlockSpec((tq, d), qrow),
            out_shape=jax.ShapeDtypeStruct((B * S, d), BF16),
            scratch_shapes=[pltpu.VMEM((tq, s_eff), F32), pltpu.VMEM((tq, s_eff), F32)],
            input_output_aliases=aliases,
            compiler_params=_cparams("arbitrary", "arbitrary"),
            name=f"prompt_attention_{j}",
        )(*args)
    return o


def _sample_scores_kernel(pt_ref, qi_ref, w_ref, kin_ref, *rest, n_pages, n_new):
    page_refs = rest[:n_pages]
    s_ref = rest[n_pages]
    kall_ref = rest[n_pages + 1]
    past = n_pages * PAGE_SIZE
    C = kall_ref.shape[1]
    for p in range(n_pages):
        kall_ref[:, p * PAGE_SIZE:(p + 1) * PAGE_SIZE] = page_refs[p][...].astype(BF16)
    kall_ref[:, past:C] = kin_ref[...].astype(BF16)
    logit = _dot(qi_ref[...], kall_ref[...])
    term = jnp.maximum(logit, 0.0) * w_ref[...]
    col = lax.broadcasted_iota(I32, (1, C), 1)
    for t in range(n_new):
        s_t = jnp.sum(term[t * IDX_HEADS:(t + 1) * IDX_HEADS, :], axis=0, keepdims=True)
        s_ref[t:t + 1, :] = jnp.where(col <= past + t, s_t, NEG)


def _sample_scores(page_table, qi_s, wi_s, ki_new_t, kidx_t, C):
    Bd, n_pages = page_table.shape
    n_new = qi_s.shape[1] // IDX_HEADS
    page_spec = lambda p: pl.BlockSpec((None, IDX_DIM, PAGE_SIZE), lambda b, pt: (pt[b, p], 0, 0))
    per_seq = lambda b, pt: (b, 0, 0)
    grid_spec = pltpu.PrefetchScalarGridSpec(
        num_scalar_prefetch=1,
        grid=(Bd,),
        in_specs=[pl.BlockSpec((None, n_new * IDX_HEADS, IDX_DIM), per_seq),
                  pl.BlockSpec((None, n_new * IDX_HEADS, 1), per_seq),
                  pl.BlockSpec((None, IDX_DIM, C - n_pages * PAGE_SIZE), per_seq)]
                 + [page_spec(p) for p in range(n_pages)],
        out_specs=pl.BlockSpec((None, n_new, C), per_seq),
        scratch_shapes=[pltpu.VMEM((IDX_DIM, C), BF16)],
    )
    return pl.pallas_call(
        functools.partial(_sample_scores_kernel, n_pages=n_pages, n_new=n_new),
        grid_spec=grid_spec,
        out_shape=jax.ShapeDtypeStruct((Bd, n_new, C), F32),
        compiler_params=_cparams("arbitrary"),
        name="sample_scores",
    )(page_table, qi_s, wi_s, ki_new_t, *([kidx_t] * n_pages))


def _sample_topk_kernel(s_ref, selx_ref, seln_ref, x_ref, sel_ref, *, topk, past, n_new):
    R, C = x_ref.shape
    flat = PAGE_SIZE * N_HEADS
    x_ref[...] = s_ref[...]
    last_c = past + lax.broadcasted_iota(I32, (R, LANES), 0) % n_new
    cols_c = lax.broadcasted_iota(I32, (R, LANES), 1)
    _topk_select(x_ref, sel_ref, lambda c0: cols_c + c0 <= last_c, topk)
    sel = sel_ref[...]
    e_r = lax.broadcasted_iota(I32, (PAGE_SIZE, flat), 0)
    e_c = lax.broadcasted_iota(I32, (PAGE_SIZE, flat), 1)
    spread = jnp.where(e_c // N_HEADS == e_r, 1.0, 0.0).astype(BF16)
    for p in range(past // PAGE_SIZE):
        selx_ref[:, p * flat:(p + 1) * flat] = _dot(
            sel[:, p * PAGE_SIZE:(p + 1) * PAGE_SIZE], spread).astype(selx_ref.dtype)
    seln_ref[...] = sel[:, past:C]


def _sample_topk(scores, topk, past, n_new, tr):
    R, C = scores.shape
    wide = past * N_HEADS
    row = lambda i: (i, 0)
    return pl.pallas_call(
        functools.partial(_sample_topk_kernel, topk=topk, past=past, n_new=n_new),
        grid=(R // tr,),
        in_specs=[pl.BlockSpec((tr, C), row)],
        out_specs=[pl.BlockSpec((tr, wide), row), pl.BlockSpec((tr, C - past), row)],
        out_shape=[jax.ShapeDtypeStruct((R, wide), BF16), jax.ShapeDtypeStruct((R, C - past), BF16)],
        scratch_shapes=[pltpu.VMEM((tr, C), F32), pltpu.VMEM((tr, C), BF16)],
        compiler_params=_cparams("arbitrary"),
        name="sample_topk",
    )(scores)


def _sample_attn_kernel(pt_ref, q_ref, selx_ref, seln_ref, kn_ref, vn_ref, *rest, pg, n_new):
    k_refs = rest[:pg]
    v_refs = rest[pg:2 * pg]
    o_ref, m_ref, l_ref, acc_ref = rest[2 * pg:]
    g = pl.program_id(1)
    rows = n_new * N_HEADS
    flat = PAGE_SIZE * N_HEADS

    @pl.when(g == 0)
    def _():
        m_ref[...] = jnp.full(m_ref.shape, NEG, F32)
        l_ref[...] = jnp.zeros(l_ref.shape, F32)
        acc_ref[...] = jnp.zeros(acc_ref.shape, F32)

    q = q_ref[...]

    def expand_rows(x):
        return jnp.concatenate(
            [jnp.broadcast_to(x[t:t + 1, :], (N_HEADS, x.shape[1])) for t in range(n_new)], axis=0)

    def update(scores, keeps, values):
        scores = [jnp.where(kp, s, NEG) for s, kp in zip(scores, keeps)]
        m_old = m_ref[...]
        m_new = m_old
        for s in scores:
            m_new = jnp.maximum(m_new, jnp.max(s, axis=-1, keepdims=True))
        alpha = jnp.exp(m_old - m_new)
        l_new = alpha * l_ref[...]
        acc = alpha * acc_ref[...]
        for s, kp, v in zip(scores, keeps, values):
            p = jnp.where(kp, jnp.exp(s - m_new), 0.0)
            l_new = l_new + jnp.sum(p, axis=-1, keepdims=True)
            acc = acc + _dot(p.astype(BF16), v)
        m_ref[...] = m_new
        l_ref[...] = l_new
        acc_ref[...] = acc

    hrow = lax.broadcasted_iota(I32, (rows, flat), 0) % N_HEADS
    hcol = lax.broadcasted_iota(I32, (rows, flat), 1) % N_HEADS
    same_head = hrow == hcol
    scores, keeps, values = [], [], []
    for p in range(pg):
        scores.append(_dot_nt(q, k_refs[p][...].astype(BF16)))
        sel_p = selx_ref[:, p * flat:(p + 1) * flat].astype(F32)
        keeps.append(same_head & (expand_rows(sel_p) > 0.5))
        values.append(v_refs[p][...].astype(BF16))
    update(scores, keeps, values)

    @pl.when(g == pl.num_programs(1) - 1)
    def _():
        s = _dot_nt(q, kn_ref[...].astype(BF16))
        r2 = lax.broadcasted_iota(I32, (rows, rows), 0)
        c2 = lax.broadcasted_iota(I32, (rows, rows), 1)
        sn = seln_ref[...].astype(F32)
        seln = jnp.concatenate(
            [jnp.broadcast_to(sn[:, j:j + 1], (n_new, N_HEADS)) for j in range(n_new)], axis=1)
        keep = (r2 % N_HEADS == c2 % N_HEADS) & (expand_rows(seln) > 0.5)
        update([s], [keep], [vn_ref[...].astype(BF16)])
        o_ref[...] = (acc_ref[...] / l_ref[...]).astype(o_ref.dtype)


def _sample_attention(page_table, q_s, selx, seln, k_new, v_new, cache_k_l, cache_v_l, pg):
    Bd, n_pages = page_table.shape
    rows = q_s.shape[1]
    n_new = rows // N_HEADS
    flat = PAGE_SIZE * N_HEADS
    per_seq = lambda b, g, pt: (b, 0, 0)
    kv_spec = lambda p: pl.BlockSpec((None, flat, HEAD_DIM), lambda b, g, pt: (pt[b, g * pg + p], 0, 0))
    grid_spec = pltpu.PrefetchScalarGridSpec(
        num_scalar_prefetch=1,
        grid=(Bd, n_pages // pg),
        in_specs=[pl.BlockSpec((None, rows, HEAD_DIM), per_seq),
                  pl.BlockSpec((None, n_new, pg * flat), lambda b, g, pt: (b, 0, g)),
                  pl.BlockSpec((None, n_new, seln.shape[2]), per_seq),
                  pl.BlockSpec((None, rows, HEAD_DIM), per_seq),
                  pl.BlockSpec((None, rows, HEAD_DIM), per_seq)]
                 + [kv_spec(p) for p in range(pg)] + [kv_spec(p) for p in range(pg)],
        out_specs=pl.BlockSpec((None, rows, HEAD_DIM), per_seq),
        scratch_shapes=[pltpu.VMEM((rows, 1), F32), pltpu.VMEM((rows, 1), F32),
                        pltpu.VMEM((rows, HEAD_DIM), F32)],
    )
    return pl.pallas_call(
        functools.partial(_sample_attn_kernel, pg=pg, n_new=n_new),
        grid_spec=grid_spec,
        out_shape=jax.ShapeDtypeStruct((Bd, rows, HEAD_DIM), BF16),
        compiler_params=_cparams("arbitrary", "arbitrary"),
        name="sample_attention",
    )(page_table, q_s, selx, seln, k_new, v_new, *([cache_k_l] * pg), *([cache_v_l] * pg))


def _out_proj_kernel(op_ref, os_ref, w_ref, xp_ref, xs_ref, y_ref, *, n_prompt_tiles):
    is_prompt = pl.program_id(0) < n_prompt_tiles
    o = jnp.where(is_prompt, op_ref[...], os_ref[...])
    x = jnp.where(is_prompt, xp_ref[...], xs_ref[...])
    y_ref[...] = x + _dot(o, w_ref[...])


def _out_proj(o_p, o_s, w, x_p, x_s, tm):
    n_p, D = x_p.shape
    T = n_p + x_s.shape[0]
    npt = n_p // tm
    prow = lambda i: (jnp.minimum(i, npt - 1), 0)
    srow = lambda i: (jnp.maximum(i - npt, 0), 0)
    return pl.pallas_call(
        functools.partial(_out_proj_kernel, n_prompt_tiles=npt),
        grid=(T // tm,),
        in_specs=[pl.BlockSpec((tm, o_p.shape[1]), prow), pl.BlockSpec((tm, o_s.shape[1]), srow),
                  pl.BlockSpec(w.shape, lambda i: (0, 0)),
                  pl.BlockSpec((tm, D), prow), pl.BlockSpec((tm, D), srow)],
        out_specs=pl.BlockSpec((tm, D), lambda i: (i, 0)),
        out_shape=jax.ShapeDtypeStruct((T, D), F32),
        compiler_params=_cparams("arbitrary"),
        name="attn_out_proj",
    )(o_p, o_s, w.astype(BF16), x_p, x_s)


def _ffn_kernel(x_ref, g_ref, wg_ref, wu_ref, wd_ref, y_ref, h_ref):
    f = pl.program_id(1)

    @pl.when(f == 0)
    def _():
        x = x_ref[...]
        h_ref[...] = _rms(x, g_ref[...]).astype(BF16)
        y_ref[...] = x

    h = h_ref[...]
    a = jax.nn.silu(_dot(h, wg_ref[...])) * _dot(h, wu_ref[...])
    y_ref[...] += _dot(a.astype(BF16), wd_ref[...])


def _ffn(x, g, w_gate, w_up, w_down, tm, tf):
    T, D = x.shape
    F = w_gate.shape[1]
    row = lambda i, f: (i, 0)
    return pl.pallas_call(
        _ffn_kernel,
        grid=(T // tm, F // tf),
        in_specs=[pl.BlockSpec((tm, D), row), pl.BlockSpec((1, D), lambda i, f: (0, 0)),
                  pl.BlockSpec((D, tf), lambda i, f: (0, f)), pl.BlockSpec((D, tf), lambda i, f: (0, f)),
                  pl.BlockSpec((tf, D), lambda i, f: (f, 0))],
        out_specs=pl.BlockSpec((tm, D), row),
        out_shape=jax.ShapeDtypeStruct((T, D), F32),
        scratch_shapes=[pltpu.VMEM((tm, D), BF16)],
        compiler_params=_cparams("arbitrary", "arbitrary"),
        name="dense_swiglu",
    )(x, g.reshape(1, D), w_gate.astype(BF16), w_up.astype(BF16), w_down.astype(BF16))


def _gmlp_kernel(x_ref, g_ref, win_ref, lng_ref, lnb_ref, ws_ref, bs_ref, wout_ref, y_ref, v_ref):
    tm, D = x_ref.shape
    W = lng_ref.shape[1]
    gd = W // GM_GROUPS
    x = x_ref[...]
    h = _rms(x, g_ref[...]).astype(BF16)
    z = _dot(h, win_ref[...])
    uv = 0.5 * z * (1.0 + lax.erf(z * math.sqrt(0.5)))
    u = uv[:, :W]
    v = uv[:, W:]
    mu = jnp.mean(v, axis=-1, keepdims=True)
    var = jnp.mean(jnp.square(v - mu), axis=-1, keepdims=True)
    v = (v - mu) * lax.rsqrt(var + EPS) * lng_ref[...] + lnb_ref[...]
    v_ref[...] = v
    vb = v.astype(BF16)
    ri = lax.broadcasted_iota(I32, (GM_CHUNK, GM_CHUNK), 0)
    ci = lax.broadcasted_iota(I32, (GM_CHUNK, GM_CHUNK), 1)
    causal = ci <= ri
    ws = [jnp.where(causal, ws_ref[g], 0.0).astype(BF16) for g in range(GM_GROUPS)]
    chunks = []
    for c in range(tm // GM_CHUNK):
        rs = slice(c * GM_CHUNK, (c + 1) * GM_CHUNK)
        mixed = jnp.concatenate(
            [_dot(ws[g], vb[rs, g * gd:(g + 1) * gd]) for g in range(GM_GROUPS)], axis=1)
        chunks.append(u[rs, :] * (mixed + bs_ref[...]))
    gated = jnp.concatenate(chunks, axis=0).astype(BF16)
    y_ref[...] = x + _dot(gated, wout_ref[...])


def _gmlp(x, g, w_in, ln_g, ln_b, w_s, b_s, w_out, tm, n_prompt_tiles, n_new):
    T, D = x.shape
    W = ln_g.shape[0]
    gd = W // GM_GROUPS
    eye = jnp.eye(GM_CHUNK // n_new, dtype=w_s.dtype)
    ws_sample = jax.vmap(lambda m: jnp.kron(eye, m))(w_s[:, :n_new, :n_new])
    ws_all = jnp.stack([w_s, ws_sample])
    bs_prompt = jnp.repeat(b_s.T, gd, axis=1)
    bs_sample = jnp.tile(bs_prompt[:n_new], (GM_CHUNK // n_new, 1))
    bs_all = jnp.stack([bs_prompt, bs_sample])
    kind = lambda i: jnp.where(i >= n_prompt_tiles, 1, 0)
    row = lambda i: (i, 0)
    const = lambda i: (0, 0)
    n_tiles = T // tm
    return pl.pallas_call(
        _gmlp_kernel,
        grid=(n_tiles,),
        in_specs=[pl.BlockSpec((tm, D), row), pl.BlockSpec((1, D), const),
                  pl.BlockSpec((D, 2 * W), const), pl.BlockSpec((1, W), const), pl.BlockSpec((1, W), const),
                  pl.BlockSpec((None, GM_GROUPS, GM_CHUNK, GM_CHUNK), lambda i: (kind(i), 0, 0, 0)),
                  pl.BlockSpec((None, GM_CHUNK, W), lambda i: (kind(i), 0, 0)),
                  pl.BlockSpec((W, D), const)],
        out_specs=[pl.BlockSpec((tm, D), row),
                   pl.BlockSpec((tm, W), lambda i: (jnp.maximum(i - n_prompt_tiles, 0), 0))],
        out_shape=[jax.ShapeDtypeStruct((T, D), F32),
                   jax.ShapeDtypeStruct(((n_tiles - n_prompt_tiles) * tm, W), F32)],
        compiler_params=_cparams("arbitrary"),
        name="gmlp_mixer",
    )(x, g.reshape(1, D), w_in.astype(BF16), ln_g.reshape(1, W), ln_b.reshape(1, W),
      ws_all, bs_all, w_out.astype(BF16))


def _router_kernel(x_ref, g_ref, wr_ref, h_ref, comb_ref, rank_ref, rankt_ref):
    tm = x_ref.shape[0]
    hf = _rms(x_ref[...], g_ref[...])
    h_ref[...] = hf.astype(BF16)
    logits = jnp.dot(hf, wr_ref[...], preferred_element_type=F32, precision=lax.Precision.HIGHEST)
    lane = lax.broadcasted_iota(I32, logits.shape, 1)
    logits = jnp.where(lane < N_EXPERTS, logits, -jnp.inf)
    m1 = jnp.max(logits, axis=-1, keepdims=True)
    i1 = jnp.min(jnp.where(logits == m1, lane, LANES), axis=-1, keepdims=True)
    rest = jnp.where(lane == i1, -jnp.inf, logits)
    m2 = jnp.max(rest, axis=-1, keepdims=True)
    i2 = jnp.min(jnp.where(rest == m2, lane, LANES), axis=-1, keepdims=True)
    e2 = jnp.exp(m2 - m1)
    den = 1.0 + e2
    comb_ref[...] = jnp.where(lane == i1, 1.0 / den, 0.0) + jnp.where(lane == i2, e2 / den, 0.0)
    routed = jnp.where((lane == i1) | (lane == i2), 1.0, 0.0).astype(BF16)
    r = lax.broadcasted_iota(I32, (tm, tm), 0)
    c = lax.broadcasted_iota(I32, (tm, tm), 1)
    rank = _dot(jnp.where(c < r, 1.0, 0.0).astype(BF16), routed)
    rank = jnp.where(routed > 0, rank, -1.0)
    rank_ref[...] = rank
    rankt_ref[...] = rank.T


def _router(x, g, w_router, tm):
    T, D = x.shape
    wr = jnp.pad(w_router, ((0, 0), (0, LANES - N_EXPERTS)))
    row = lambda i: (i, 0)
    colb = lambda i: (0, i)
    const = lambda i: (0, 0)
    return pl.pallas_call(
        _router_kernel,
        grid=(T // tm,),
        in_specs=[pl.BlockSpec((tm, D), row), pl.BlockSpec((1, D), const), pl.BlockSpec((D, LANES), const)],
        out_specs=[pl.BlockSpec((tm, D), row), pl.BlockSpec((tm, LANES), row),
                   pl.BlockSpec((tm, LANES), row), pl.BlockSpec((LANES, tm), colb)],
        out_shape=[jax.ShapeDtypeStruct((T, D), BF16), jax.ShapeDtypeStruct((T, LANES), F32),
                   jax.ShapeDtypeStruct((T, LANES), F32), jax.ShapeDtypeStruct((LANES, T), F32)],
        compiler_params=_cparams("arbitrary"),
        name="moe_router",
    )(x, g.reshape(1, D), wr)


def _moe_kernel(cnt_ref, h_ref, comb_ref, rank_ref, rankt_ref, wg_ref, wu_ref, wd_ref, acc_ref, *rest,
                e, n_prompt_tiles, n_chunks, n_out):
    outs = rest[:n_out]
    wg_s, wu_s, wd_s = rest[n_out:n_out + 3]
    y_ref = outs[0] if n_out == 1 else rest[n_out + 3]
    s = pl.program_id(0)
    tm = h_ref.shape[0]

    tf = wg_ref.shape[1]
    for c in range(n_chunks):
        @pl.when(s == c)
        def _(c=c):
            wg_s[:, c * tf:(c + 1) * tf] = wg_ref[...].astype(BF16)
            wu_s[:, c * tf:(c + 1) * tf] = wu_ref[...].astype(BF16)
            wd_s[c * tf:(c + 1) * tf, :] = wd_ref[...].astype(BF16)

    @pl.when(s >= n_chunks - 1)
    def _():
        i = s - (n_chunks - 1)
        gate = comb_ref[:, e:e + 1]
        rank_c = rank_ref[:, e:e + 1]
        rank_r = rankt_ref[e:e + 1, :]
        n = cnt_ref[i]

        def block(first, rows):
            base = first.astype(F32)
            slot_r = lax.broadcasted_iota(I32, (rows, tm), 0).astype(F32) + base
            take = jnp.where(rank_r == slot_r, 1.0, 0.0).astype(BF16)
            xs = _dot(take, h_ref[...]).astype(BF16)
            a = jax.nn.silu(_dot(xs, wg_s[...])) * _dot(xs, wu_s[...])
            y = _dot(a.astype(BF16), wd_s[...])
            slot_c = lax.broadcasted_iota(I32, (tm, rows), 1).astype(F32) + base
            put = jnp.where(rank_c == slot_c, 1.0, 0.0).astype(BF16)
            return gate * _dot(put, y.astype(BF16))

        @pl.when(n == 0)
        def _():
            y_ref[...] = acc_ref[...]

        smaller = 0
        for rows in MOE_BLOCK_ROWS:
            @pl.when((n > smaller) & (n <= rows))
            def _(rows=rows):
                y_ref[...] = acc_ref[...] + block(jnp.int32(0), rows)
            smaller = rows

        @pl.when(n > smaller)
        def _():
            y_ref[...] = acc_ref[...]

            def step(jb, carry):
                y_ref[...] += block(jb * MOE_BLOCK_ROWS[0], MOE_BLOCK_ROWS[0])
                return carry
            lax.fori_loop(0, (n + MOE_BLOCK_ROWS[0] - 1) // MOE_BLOCK_ROWS[0], step, 0)

        if n_out > 1:
            @pl.when(i < n_prompt_tiles)
            def _():
                outs[0][...] = y_ref[...]

            @pl.when(i >= n_prompt_tiles)
            def _():
                outs[1][...] = y_ref[...]


def _moe_expert(e, acc, h, comb, rank, rankt, counts_e, w_gate, w_up, w_down, tm, n_prompt_tiles, last):
    T, D = acc.shape
    F = w_gate.shape[2]
    tf = _largest_divisor(F, (512, 256, 128))
    nc = F // tf
    npt = n_prompt_tiles
    tile = lambda s: jnp.maximum(s - (nc - 1), 0)
    chunk = lambda s: jnp.minimum(s, nc - 1)
    tok = lambda s, cnt: (tile(s), 0)
    if last:
        out_specs = [pl.BlockSpec((tm, D), lambda s, cnt: (jnp.minimum(tile(s), npt - 1), 0)),
                     pl.BlockSpec((tm, D), lambda s, cnt: (jnp.maximum(tile(s) - npt, 0), 0))]
        out_shape = [jax.ShapeDtypeStruct((npt * tm, D), F32), jax.ShapeDtypeStruct((T - npt * tm, D), F32)]
        aliases = {}
    else:
        out_specs = pl.BlockSpec((tm, D), tok)
        out_shape = jax.ShapeDtypeStruct((T, D), F32)
        aliases = {8: 0}
    grid_spec = pltpu.PrefetchScalarGridSpec(
        num_scalar_prefetch=1,
        grid=(T // tm + nc - 1,),
        in_specs=[pl.BlockSpec((tm, D), tok), pl.BlockSpec((tm, LANES), tok), pl.BlockSpec((tm, LANES), tok),
                  pl.BlockSpec((LANES, tm), lambda s, cnt: (0, tile(s))),
                  pl.BlockSpec((None, D, tf), lambda s, cnt: (e, 0, chunk(s))),
                  pl.BlockSpec((None, D, tf), lambda s, cnt: (e, 0, chunk(s))),
                  pl.BlockSpec((None, tf, D), lambda s, cnt: (e, chunk(s), 0)),
                  pl.BlockSpec((tm, D), tok)],
        out_specs=out_specs,
        scratch_shapes=[pltpu.VMEM((D, F), BF16), pltpu.VMEM((D, F), BF16),
                        pltpu.VMEM((F, D), BF16)] + ([pltpu.VMEM((tm, D), F32)] if last else []),
    )
    return pl.pallas_call(
        functools.partial(_moe_kernel, e=e, n_prompt_tiles=npt, n_chunks=nc, n_out=2 if last else 1),
        grid_spec=grid_spec,
        out_shape=out_shape,
        input_output_aliases=aliases,
        compiler_params=_cparams("arbitrary"),
        name=f"moe_expert_{e}",
    )(counts_e, h, comb, rank, rankt, w_gate, w_up, w_down, acc)


def _token_tile(n_prompt, n_sample):
    for tm in (512, 256, 128):
        if n_prompt % tm == 0 and n_sample % tm == 0:
            return tm
    raise ValueError("token counts must be multiples of 128")


def _largest_divisor(n, candidates):
    for c in candidates:
        if n % c == 0:
            return c
    return n


def kernel(x_prompt, x_sample, cache_k, cache_v, cache_kidx, page_table, norm_mix, norm_ffn, attn_w_in, attn_g_q, attn_g_k, attn_g_kidx, attn_w_out, gm_w_in, gm_ln_g, gm_ln_b, gm_w_s, gm_b_s, gm_w_out, ffn_w_gate, ffn_w_up, ffn_w_down, moe_w_router, moe_w_gate, moe_w_up, moe_w_down):
    B, S, D = x_prompt.shape
    Bd, n_new, _ = x_sample.shape
    n_pages = page_table.shape[1]
    past = n_pages * PAGE_SIZE
    assert norm_mix.shape[0] == 2 and D == N_HEADS * HEAD_DIM
    n_p, n_s = B * S, Bd * n_new
    T = n_p + n_s
    tm = _token_tile(n_p, n_s)
    npt = n_p // tm
    topk_prompt = min(IDX_TOPK, S // 4)
    topk_sample = min(IDX_TOPK, (past + n_new) // 4)
    d = N_HEADS * HEAD_DIM
    x_p = x_prompt.reshape(n_p, D)
    x_s = x_sample.reshape(n_s, D)

    q, kb, vb, qi, kw, k_p, v_p, k_s, v_s = _attn_project(
        x_p, x_s, norm_mix[0], attn_w_in[0], attn_g_q[0], attn_g_k[0], attn_g_kidx[0], tm)
    tq = _largest_divisor(S, (256, 128, 64))
    o_p = _prompt_attention(q, qi, kw, kb, vb, B, S, tq, topk_prompt)

    C = past + LANES
    qi_s = qi[n_p:].reshape(Bd, n_new * IDX_HEADS, IDX_DIM)
    wi_s = kw[n_p:, IDX_DIM:IDX_DIM + IDX_HEADS].reshape(Bd, n_new * IDX_HEADS, 1)
    ki_new_t = jnp.pad(jnp.swapaxes(kw[n_p:, :IDX_DIM].reshape(Bd, n_new, IDX_DIM), 1, 2),
                       ((0, 0), (0, 0), (0, LANES - n_new)))
    scores = _sample_scores(page_table, qi_s, wi_s, ki_new_t, jnp.swapaxes(cache_kidx[0], 1, 2), C)
    selx, seln = _sample_topk(scores.reshape(n_s, C), topk_sample, past, n_new,
                              _largest_divisor(n_s, (128, 64, 32, 16)))
    flat = PAGE_SIZE * N_HEADS
    n_pool = cache_k.shape[1]
    pg = _largest_divisor(n_pages, tuple(range(MAX_PAGES_PER_STEP, 0, -1)))
    o_s = _sample_attention(
        page_table, q[n_p:].reshape(Bd, n_new * N_HEADS, HEAD_DIM),
        selx.reshape(Bd, n_new, past * N_HEADS), seln.reshape(Bd, n_new, LANES),
        k_s.reshape(Bd, n_new * N_HEADS, HEAD_DIM), v_s.reshape(Bd, n_new * N_HEADS, HEAD_DIM),
        cache_k[0].reshape(n_pool, flat, HEAD_DIM), cache_v[0].reshape(n_pool, flat, HEAD_DIM), pg)
    x = _out_proj(o_p, o_s.reshape(n_s, d), attn_w_out[0], x_p, x_s, tm)
    x = _ffn(x, norm_ffn[0], ffn_w_gate[0], ffn_w_up[0], ffn_w_down[0], tm,
             _largest_divisor(ffn_w_gate.shape[2], (1408, 1024, 512, 256, 128)))

    x, gm_v = _gmlp(x, norm_mix[1], gm_w_in[0], gm_ln_g[0], gm_ln_b[0], gm_w_s[0], gm_b_s[0], gm_w_out[0],
                    tm, npt, n_new)
    h, comb, rank, rankt = _router(x, norm_ffn[1], moe_w_router[0], tm)
    counts = jnp.sum((rankt[:N_EXPERTS] >= 0.0).reshape(N_EXPERTS, T // tm, tm), axis=-1).astype(I32)
    wg, wu, wd = moe_w_gate[0], moe_w_up[0], moe_w_down[0]
    for e in range(N_EXPERTS):
        x = _moe_expert(e, x, h, comb, rank, rankt, counts[e], wg, wu, wd, tm, npt, e == N_EXPERTS - 1)
    y_p, y_s = x

    n_pp = n_p // PAGE_SIZE
    return (y_p.reshape(B, S, D), y_s.reshape(Bd, n_new, D),
            k_p.reshape(1, n_pp, PAGE_SIZE, N_HEADS, HEAD_DIM),
            v_p.reshape(1, n_pp, PAGE_SIZE, N_HEADS, HEAD_DIM),
            kw[:n_p, :IDX_DIM].reshape(1, n_pp, PAGE_SIZE, IDX_DIM),
            k_s.reshape(1, Bd, n_new, N_HEADS, HEAD_DIM),
            v_s.reshape(1, Bd, n_new, N_HEADS, HEAD_DIM),
            kw[n_p:, :IDX_DIM].reshape(1, Bd, n_new, IDX_DIM),
            gm_v.reshape(1, Bd, n_new, gm_ln_g.shape[1]))
```
